```python
import jax, jax.numpy as jnp
from jax import lax
import numpy as np

D_MODEL = 2048
BATCH = 16
SEQ = 256
DEPTH = 4
DEC_BATCH = 8
DEC_SEQ = 1024
PAST_LEN = 512

GRID_W = 64
N_MIXERS = 4
N_RET = (DEPTH + 3) // N_MIXERS
N_FNET = (DEPTH + 2) // N_MIXERS
N_CONV = (DEPTH + 1) // N_MIXERS
N_POOL = DEPTH // N_MIXERS
RET_HEADS = 8
RET_DK = D_MODEL // RET_HEADS
RET_DV = 2 * RET_DK
RET_CHUNK = 128
ROPE_BASE = 10000.0
FNET_GROUPS = 4
CONV_WIDTH = 31
POOL_WINDOWS = (2, 4, 8, 16)
POOL_GROUP = D_MODEL // len(POOL_WINDOWS)
N_EXPERTS = 16
CAPACITY_FACTOR = 2
D_FF_EXPERT = 2 * D_MODEL
N_MOD = 6
EPS = 1e-6

kernel_name = 'hybrid_retention_fnet_conformer_pool_ecmoe_diffusion_step'


def rmsnorm(x, g):
    xf = x.astype(jnp.float32)
    y = xf * lax.rsqrt(jnp.mean(xf * xf, axis=-1, keepdims=True) + EPS)
    return (y * g.astype(jnp.float32)).astype(x.dtype)


def rope_1d(x, pos):
    half = x.shape[-1] // 2
    freqs = ROPE_BASE ** (-jnp.arange(half, dtype=jnp.float32) / half)
    ang = pos[:, None] * freqs[None, :]
    cos = jnp.cos(ang).astype(x.dtype)
    sin = jnp.sin(ang).astype(x.dtype)
    x1, x2 = x[..., :half], x[..., half:]
    return jnp.concatenate([x1 * cos - x2 * sin, x1 * sin + x2 * cos], axis=-1)


def rotary_2d(x):
    rows = x.shape[2] // GRID_W
    r, col = jnp.meshgrid(jnp.arange(rows, dtype=jnp.float32), jnp.arange(GRID_W, dtype=jnp.float32), indexing='ij')
    half = x.shape[-1] // 2
    return jnp.concatenate([rope_1d(x[..., :half], r.reshape(-1)), rope_1d(x[..., half:], col.reshape(-1))], axis=-1)


def retention_scan(q, k, v, log_gamma, init_state):
    b, h, s, _ = q.shape
    dv = v.shape[-1]
    n = s // RET_CHUNK
    to_chunks = lambda t: t.astype(jnp.float32).reshape(b, h, n, RET_CHUNK, t.shape[-1]).transpose(2, 0, 1, 3, 4)
    idx = jnp.arange(RET_CHUNK, dtype=jnp.float32)
    diff = idx[:, None] - idx[None, :]
    intra = jnp.where(diff >= 0, jnp.exp(log_gamma[:, None, None] * jnp.maximum(diff, 0.0)), 0.0)
    q_dec = jnp.exp(log_gamma[:, None] * (idx + 1.0))[..., None]
    k_dec = jnp.exp(log_gamma[:, None] * (RET_CHUNK - 1.0 - idx))[..., None]
    chunk_dec = jnp.exp(log_gamma * RET_CHUNK)[:, None, None]

    def step(state, inp):
        qc, kc, vc = inp
        scores = jnp.einsum('bhid,bhjd->bhij', qc, kc) * intra
        y = jnp.einsum('bhij,bhjv->bhiv', scores, vc) + jnp.einsum('bhid,bhdv->bhiv', qc * q_dec, state)
        state = state * chunk_dec + jnp.einsum('bhjd,bhjv->bhdv', kc * k_dec, vc)
        return state, y

    final, ys = lax.scan(step, init_state.astype(jnp.float32), (to_chunks(q), to_chunks(k), to_chunks(v)))
    return ys.transpose(1, 2, 0, 3, 4).reshape(b, h, s, dv), final


def retention_mixer(h, w_in, w_out, decay_logit, gn_gain, init_state, positional):
    b, s, _ = h.shape
    qk = RET_HEADS * RET_DK
    vd = RET_HEADS * RET_DV
    q, k, v, g = jnp.split(h @ w_in, [qk, 2 * qk, 2 * qk + vd], axis=-1)
    heads = lambda t, d: t.reshape(b, s, RET_HEADS, d).transpose(0, 2, 1, 3)
    q, k, v = heads(q, RET_DK), heads(k, RET_DK), heads(v, RET_DV)
    if positional:
        q, k = rotary_2d(q), rotary_2d(k)
    k = k * (RET_DK ** -0.5)
    log_g = jax.nn.log_sigmoid(decay_logit.astype(jnp.float32))
    y_f, s_f = retention_scan(q, k, v, log_g[0], init_state[:, 0])
    flip = lambda t: jnp.flip(t, axis=2)
    y_b, s_b = retention_scan(flip(q), flip(k), flip(v), log_g[1], init_state[:, 1])
    y = y_f + flip(y_b)
    mu = jnp.mean(y, axis=-1, keepdims=True)
    var = jnp.mean(jnp.square(y - mu), axis=-1, keepdims=True)
    y = ((y - mu) * lax.rsqrt(var + EPS)).transpose(0, 2, 1, 3).reshape(b, s, vd) * gn_gain.astype(jnp.float32)
    out = (jax.nn.silu(g.astype(jnp.float32)) * y).astype(h.dtype) @ w_out
    return out, jnp.stack([s_f, s_b], axis=1).astype(h.dtype)


def fnet_mixer(h, w):
    b, s, d = h.shape
    hg = h.astype(jnp.float32).reshape(b, s, FNET_GROUPS, d // FNET_GROUPS)
    f = jnp.fft.fftn(hg, axes=(1, 3), norm='ortho').real
    return f.reshape(b, s, d).astype(h.dtype) @ w


def conformer_conv(h, w1, w_dw, b_dw, ln_g, ln_b, w2):
    d = h.shape[-1]
    a = h @ w1
    u = a[..., :d] * jax.nn.sigmoid(a[..., d:])
    u = lax.conv_general_dilated(u, w_dw[:, None, :], window_strides=(1,),
                                 padding=[(CONV_WIDTH // 2, CONV_WIDTH // 2)],
                                 dimension_numbers=('NWC', 'WIO', 'NWC'),
                                 feature_group_count=d) + b_dw
    uf = u.astype(jnp.float32)
    mu = jnp.mean(uf, axis=-1, keepdims=True)
    var = jnp.mean(jnp.square(uf - mu), axis=-1, keepdims=True)
    uf = (uf - mu) * lax.rsqrt(var + EPS) * ln_g.astype(jnp.float32) + ln_b.astype(jnp.float32)
    return jax.nn.silu(uf).astype(h.dtype) @ w2


def pool_mixer(h, w_pool, scale):
    b, s, d = h.shape
    hf = h.astype(jnp.float32)
    csum = jnp.concatenate([jnp.zeros((b, 1, d), jnp.float32), jnp.cumsum(hf, axis=1)], axis=1)
    t = jnp.arange(s)
    outs = []
    for gi, w in enumerate(POOL_WINDOWS):
        lo = jnp.clip(t - w // 2, 0, s - 1)
        hi = jnp.clip(t + w // 2 - 1, 0, s - 1)
        cs = slice(gi * POOL_GROUP, (gi + 1) * POOL_GROUP)
        win_sum = csum[:, hi + 1, cs] - csum[:, lo, cs]
        cnt = (hi - lo + 1).astype(jnp.float32)[None, :, None]
        outs.append(win_sum / cnt - hf[..., cs])
    pooled = jnp.concatenate(outs, axis=-1).astype(h.dtype).reshape(b, s, len(POOL_WINDOWS), POOL_GROUP)
    y = jnp.einsum('bsgc,gcd->bsgd', pooled, w_pool).reshape(b, s, d)
    return y * scale


def expert_choice_ffn(h, w_router, w_gate, w_up, w_down):
    b, s, d = h.shape
    n = b * s
    cap = CAPACITY_FACTOR * n // N_EXPERTS
    x = h.reshape(n, d)
    aff = jax.nn.softmax((x @ w_router).astype(jnp.float32), axis=-1)
    gate, idx = lax.top_k(aff.T, cap)
    xe = x[idx]
    hid = jax.nn.silu(jnp.einsum('ecd,edf->ecf', xe, w_gate)) * jnp.einsum('ecd,edf->ecf', xe, w_up)
    ye = jnp.einsum('ecf,efd->ecd', hid, w_down) * gate[..., None].astype(h.dtype)
    y = jnp.zeros((n, d), h.dtype).at[idx.reshape(-1)].add(ye.reshape(-1, d))
    return y.reshape(b, s, d)


def run_trunk(x, cond, ret_init, positional, w_mod, b_mod, g_norm, ret_w_in, ret_w_out, ret_decay, ret_gn,
              fnet_w, conv_w1, conv_wdw, conv_bdw, conv_ln_g, conv_ln_b, conv_w2, pool_w, pool_scale,
              moe_router, moe_w_gate, moe_w_up, moe_w_down):
    ret_states = []
    for i in range(DEPTH):
        mod = (jax.nn.silu(cond.astype(jnp.float32)).astype(x.dtype) @ w_mod[i] + b_mod[i])[:, None, :]
        sh_a, sc_a, gt_a, sh_f, sc_f, gt_f = jnp.split(mod, N_MOD, axis=-1)
        h = rmsnorm(x, g_norm[i, 0]) * (1 + sc_a) + sh_a
        kind, j = i % N_MIXERS, i // N_MIXERS
        if kind == 0:
            out, st = retention_mixer(h, ret_w_in[j], ret_w_out[j], ret_decay[j], ret_gn[j], ret_init[:, j], positional)
            ret_states.append(st)
        elif kind == 1:
            out = fnet_mixer(h, fnet_w[j])
        elif kind == 2:
            out = conformer_conv(h, conv_w1[j], conv_wdw[j], conv_bdw[j], conv_ln_g[j], conv_ln_b[j], conv_w2[j])
        else:
            out = pool_mixer(h, pool_w[j], pool_scale[j])
        x = x + gt_a * rmsnorm(out, g_norm[i, 1])
        h = rmsnorm(x, g_norm[i, 2]) * (1 + sc_f) + sh_f
        ffn = expert_choice_ffn(h, moe_router[i], moe_w_gate[i], moe_w_up[i], moe_w_down[i])
        x = x + gt_f * rmsnorm(ffn, g_norm[i, 3])
    return x, jnp.stack(ret_states, axis=1)


def setup_inputs(seed: int = 0) -> dict:
    key = jax.random.key(seed)
    ks = iter(jax.random.split(key, 32))
    nrm = lambda shape, std: jax.random.normal(next(ks), shape, jnp.float32) * std
    d = D_MODEL
    in_w = 2 * RET_HEADS * RET_DK + 2 * RET_HEADS * RET_DV
    vd = RET_HEADS * RET_DV
    decay_base = jnp.asarray(np.log(2.0 ** (5 + np.arange(RET_HEADS)) - 1.0).astype(np.float32))
    state_std = (PAST_LEN ** 0.5) * (RET_DK ** -0.5) * 0.5
    return {
        'x_prompt': nrm((BATCH, SEQ, d), 1.0),
        'x_sample': nrm((DEC_BATCH, DEC_SEQ, d), 1.0),
        'state_ret': nrm((DEC_BATCH, N_RET, 2, RET_HEADS, RET_DK, RET_DV), state_std),
        'c': nrm((DEC_BATCH, d), 1.0),
        'c_ctx': nrm((d,), 1.0),
        'w_mod': nrm((DEPTH, d, N_MOD * d), 0.2 * d ** -0.5),
        'b_mod': nrm((DEPTH, N_MOD * d), 0.1),
        'g_norm': 1.0 + nrm((DEPTH, 4, d), 0.05),
        'ret_w_in': nrm((N_RET, d, in_w), d ** -0.5),
        'ret_w_out': nrm((N_RET, vd, d), vd ** -0.5),
        'ret_decay': decay_base[None, None, :] + nrm((N_RET, 2, RET_HEADS), 0.05),
        'ret_gn': 1.0 + nrm((N_RET, vd), 0.05),
        'fnet_w': nrm((N_FNET, d, d), d ** -0.5),
        'conv_w1': nrm((N_CONV, d, 2 * d), d ** -0.5),
        'conv_wdw': nrm((N_CONV, CONV_WIDTH, d), CONV_WIDTH ** -0.5),
        'conv_bdw': nrm((N_CONV, d), 0.02),
        'conv_ln_g': 1.0 + nrm((N_CONV, d), 0.05),
        'conv_ln_b': nrm((N_CONV, d), 0.02),
        'conv_w2': nrm((N_CONV, d, d), d ** -0.5),
        'pool_w': nrm((N_POOL, len(POOL_WINDOWS), POOL_GROUP, POOL_GROUP), POOL_GROUP ** -0.5),
        'pool_scale': 1.0 + nrm((N_POOL, d), 0.1),
        'moe_router': nrm((DEPTH, d, N_EXPERTS), d ** -0.5),
        'moe_w_gate': nrm((DEPTH, N_EXPERTS, d, D_FF_EXPERT), d ** -0.5),
        'moe_w_up': nrm((DEPTH, N_EXPERTS, d, D_FF_EXPERT), d ** -0.5),
        'moe_w_down': nrm((DEPTH, N_EXPERTS, D_FF_EXPERT, d), D_FF_EXPERT ** -0.5),
    }


def reference(x_prompt, x_sample, state_ret, c, c_ctx, w_mod, b_mod, g_norm, ret_w_in, ret_w_out, ret_decay,
              ret_gn, fnet_w, conv_w1, conv_wdw, conv_bdw, conv_ln_g, conv_ln_b, conv_w2, pool_w, pool_scale,
              moe_router, moe_w_gate, moe_w_up, moe_w_down):
    wts = (w_mod, b_mod, g_norm, ret_w_in, ret_w_out, ret_decay, ret_gn, fnet_w, conv_w1, conv_wdw, conv_bdw,
           conv_ln_g, conv_ln_b, conv_w2, pool_w, pool_scale, moe_router, moe_w_gate, moe_w_up, moe_w_down)
    ctx_init = jnp.zeros((x_prompt.shape[0], N_RET, 2, RET_HEADS, RET_DK, RET_DV), x_prompt.dtype)
    y_prompt, new_state_ret = run_trunk(x_prompt, c_ctx[None, :], ctx_init, False, *wts)
    y_sample, _ = run_trunk(x_sample, c, state_ret, True, *wts)
    return (y_prompt, y_sample, new_state_ret)
```

```python
import functools

import jax
import jax.numpy as jnp
import numpy as np
from jax import lax
from jax.experimental import pallas as pl
from jax.experimental.pallas import tpu as pltpu

F32 = jnp.float32
BF16 = jnp.bfloat16

D = 2048
BATCH, SEQ = 16, 256
DEC_BATCH, DEC_SEQ = 8, 1024
N_PROMPT = BATCH * SEQ
N_SAMPLE = DEC_BATCH * DEC_SEQ
N_TOK = N_PROMPT + N_SAMPLE
DEPTH = 4
GRID_W = 64
HEADS = 8
DK = D // HEADS
DV = 2 * DK
CHUNK = 128
ROPE_BASE = 10000.0
FNET_GROUPS = 4
FNET_C = D // FNET_GROUPS
CONV_WIDTH = 31
POOL_WINDOWS = (2, 4, 8, 16)
POOL_GROUP = D // len(POOL_WINDOWS)
N_EXPERTS = 16
CAP_P = 2 * N_PROMPT // N_EXPERTS
CAP_S = 2 * N_SAMPLE // N_EXPERTS
CAP_T = CAP_P + CAP_S
D_FF = 2 * D
N_MOD = 6
EPS = 1e-6
IN_W = 2 * HEADS * DK + 2 * HEADS * DV
VD = HEADS * DV
MOD_ROWS = 16
ROUTER_LANES = 128

VMEM_LIMIT = 56 * 1024 * 1024


def _cparams(sem):
    return pltpu.CompilerParams(dimension_semantics=sem, vmem_limit_bytes=VMEM_LIMIT)


def _sigmoid(x):
    return 1.0 / (1.0 + jnp.exp(-x))


def _dot(a, b):
    return jnp.dot(a, b, preferred_element_type=F32)


def _mod_row(row0):
    return jnp.where(row0 < N_PROMPT, 0, 1 + (row0 - N_PROMPT) // DEC_SEQ)


def _rms(v, g):
    ms = jnp.mean(v * v, axis=-1, keepdims=True)
    return v * lax.rsqrt(ms + EPS) * g


def _mod_kernel(c_ref, w_ref, b_ref, o_ref):
    c = c_ref[...]
    s = (c * _sigmoid(c)).astype(BF16)
    o_ref[...] = _dot(s, w_ref[...].astype(BF16)) + b_ref[...]


def _mod_table(cond_all, w_mod, b_mod):
    tn = 1024
    out = pl.pallas_call(
        _mod_kernel,
        grid=(DEPTH, N_MOD * D // tn),
        in_specs=[
            pl.BlockSpec((MOD_ROWS, D), lambda l, j: (0, 0)),
            pl.BlockSpec((None, D, tn), lambda l, j: (l, 0, j)),
            pl.BlockSpec((None, 1, tn), lambda l, j: (l, 0, j)),
        ],
        out_specs=pl.BlockSpec((None, MOD_ROWS, tn), lambda l, j: (l, 0, j)),
        out_shape=jax.ShapeDtypeStruct((DEPTH, MOD_ROWS, N_MOD * D), F32),
        compiler_params=_cparams(("parallel", "parallel")),
        name="mod_table",
    )(cond_all, w_mod, b_mod.reshape(DEPTH, 1, N_MOD * D))
    return out.reshape(DEPTH, MOD_ROWS * N_MOD, 1, D)


def _post_pre(out, x, gpost, gt, pre, router):
    xn = x + gt * _rms(out, gpost)
    if pre is None:
        return xn, None, None
    gpre, sc, sh = pre
    h = _rms(xn, gpre) * (1.0 + sc) + sh
    if router is None:
        return xn, h, None
    wrh, wrl = router
    hh = h.astype(BF16)
    hl = (h - hh.astype(F32)).astype(BF16)
    lg = _dot(hh, wrh) + _dot(hh, wrl) + _dot(hl, wrh)
    lane = lax.broadcasted_iota(jnp.int32, lg.shape, 1)
    lg = jnp.where(lane < N_EXPERTS, lg, -jnp.inf)
    e = jnp.exp(lg - jnp.max(lg, axis=-1, keepdims=True))
    aff = e / jnp.sum(e, axis=-1, keepdims=True)
    return xn, h, aff


def _epilogue_from_refs(out, refs, has_pre, has_router):
    it = iter(refs)
    x_ref, gpost_ref, gt_ref = next(it), next(it), next(it)
    pre = router = None
    if has_pre:
        pre = (next(it)[...], next(it)[...], next(it)[...])
    if has_router:
        router = (next(it)[...], next(it)[...])
    xo_ref = next(it)
    xn, h, aff = _post_pre(out, x_ref[...], gpost_ref[...], gt_ref[...], pre, router)
    xo_ref[...] = xn
    if has_pre:
        h_ref = next(it)
        h_ref[...] = h.astype(h_ref.dtype)
    if has_router:
        aff_ref = next(it)
        aff_ref[...] = aff


def _epilogue_specs(tm, row_imap, which_gt, which_pre, has_pre, has_router, h_dtype, n_rows=N_TOK,
                    mod_row0=None):
    if mod_row0 is None:
        mod_row0 = lambda *idx: row_imap(*idx) * tm

    def rows(*idx):
        return (row_imap(*idx), 0)

    def const(*idx):
        return (0, 0)

    def modmap(which):
        return lambda *idx: (_mod_row(mod_row0(*idx)) * N_MOD + which, 0, 0)

    in_specs = [pl.BlockSpec((tm, D), rows), pl.BlockSpec((1, D), const),
                pl.BlockSpec((None, 1, D), modmap(which_gt))]
    if has_pre:
        in_specs += [pl.BlockSpec((1, D), const),
                     pl.BlockSpec((None, 1, D), modmap(which_pre + 1)),
                     pl.BlockSpec((None, 1, D), modmap(which_pre))]
    if has_router:
        in_specs += [pl.BlockSpec((D, ROUTER_LANES), const), pl.BlockSpec((D, ROUTER_LANES), const)]
    out_specs = [pl.BlockSpec((tm, D), rows)]
    out_shape = [jax.ShapeDtypeStruct((n_rows, D), F32)]
    if has_pre:
        out_specs.append(pl.BlockSpec((tm, D), rows))
        out_shape.append(jax.ShapeDtypeStruct((n_rows, D), h_dtype))
    if has_router:
        out_specs.append(pl.BlockSpec((tm, ROUTER_LANES), rows))
        out_shape.append(jax.ShapeDtypeStruct((n_rows, ROUTER_LANES), F32))
    return in_specs, out_specs, out_shape


def _epilogue_args(x, gpost, mod_post, gpre, mod_pre, router):
    args = [x, gpost, mod_post]
    if gpre is not None:
        args += [gpre, mod_pre, mod_pre]
    if router is not None:
        args += list(router)
    return args


def _prenorm_kernel(x_ref, g_ref, sc_ref, sh_ref, h_ref):
    h = _rms(x_ref[...], g_ref[...]) * (1.0 + sc_ref[...]) + sh_ref[...]
    h_ref[...] = h.astype(h_ref.dtype)


def _prenorm(x, g, mod_l, which_pre, h_dtype):
    tm = 512
    modmap = lambda which: (lambda i: (_mod_row(i * tm) * N_MOD + which, 0, 0))
    return pl.pallas_call(
        _prenorm_kernel,
        grid=(N_TOK // tm,),
        in_specs=[pl.BlockSpec((tm, D), lambda i: (i, 0)), pl.BlockSpec((1, D), lambda i: (0, 0)),
                  pl.BlockSpec((None, 1, D), modmap(which_pre + 1)),
                  pl.BlockSpec((None, 1, D), modmap(which_pre))],
        out_specs=pl.BlockSpec((tm, D), lambda i: (i, 0)),
        out_shape=jax.ShapeDtypeStruct((N_TOK, D), h_dtype),
        compiler_params=_cparams(("parallel",)),
        name="prenorm",
    )(x, g, mod_l, mod_l)


def _post_kernel(y_ref, *refs, has_pre, has_router):
    _epilogue_from_refs(y_ref[...], refs, has_pre, has_router)


def _post(y, x, gpost, mod_l, which_gt, gpre, mod_pre, which_pre, h_dtype):
    tm = 512
    has_pre = gpre is not None
    in_specs, out_specs, out_shape = _epilogue_specs(tm, lambda i: i, which_gt, which_pre, has_pre,
                                                     False, h_dtype)
    res = pl.pallas_call(
        functools.partial(_post_kernel, has_pre=has_pre, has_router=False),
        grid=(N_TOK // tm,),
        in_specs=[pl.BlockSpec((tm, D), lambda i: (i, 0))] + in_specs,
        out_specs=out_specs, out_shape=out_shape,
        compiler_params=_cparams(("parallel",)),
        name="post_pre",
    )(y, *_epilogue_args(x, gpost, mod_l, gpre, mod_pre, None))
    return res if has_pre else (res[0], None)


def _mm_ws_kernel(h_ref, *refs, glu):
    if glu:
        wa_ref, wb_ref, o_ref, wa_s, wb_s = refs
    else:
        wa_ref, o_ref, wa_s = refs

    @pl.when(pl.program_id(1) == 0)
    def _():
        wa_s[...] = wa_ref[...].astype(BF16)
        if glu:
            wb_s[...] = wb_ref[...].astype(BF16)

    h = h_ref[...]
    a = _dot(h, wa_s[...])
    if glu:
        a = a * _sigmoid(_dot(h, wb_s[...]))
    o_ref[...] = a.astype(o_ref.dtype)


def _mm_ws(h, w, n_out, glu, tm, tn, out_dtype, name):
    k = h.shape[1]
    nj = n_out // tn
    in_specs = [pl.BlockSpec((tm, k), lambda j, i: (i, 0)), pl.BlockSpec((k, tn), lambda j, i: (0, j))]
    args = [h, w]
    scratch = [pltpu.VMEM((k, tn), BF16)]
    if glu:
        in_specs.append(pl.BlockSpec((k, tn), lambda j, i: (0, nj + j)))
        args.append(w)
        scratch.append(pltpu.VMEM((k, tn), BF16))
    return pl.pallas_call(
        functools.partial(_mm_ws_kernel, glu=glu),
        grid=(nj, h.shape[0] // tm),
        in_specs=in_specs,
        out_specs=pl.BlockSpec((tm, tn), lambda j, i: (i, j)),
        out_shape=jax.ShapeDtypeStruct((h.shape[0], n_out), out_dtype),
        scratch_shapes=scratch,
        compiler_params=_cparams(("parallel", "arbitrary")),
        name=name,
    )(*args)


def _mm_post_kernel(a_ref, w_ref, *refs, nk, has_pre, has_router):
    acc_ref = refs[-1]
    k = pl.program_id(1)

    @pl.when(k == 0)
    def _():
        acc_ref[...] = jnp.zeros_like(acc_ref)

    acc_ref[...] += _dot(a_ref[...], w_ref[...].astype(BF16))

    @pl.when(k == nk - 1)
    def _():
        _epilogue_from_refs(acc_ref[...], refs[:-1], has_pre, has_router)


def _mm_post(a, w, x, gpost, mod_l, which_gt, gpre, which_pre, router, h_dtype, name):
    tm, tk = 512, 512
    kdim = a.shape[1]
    nk = kdim // tk
    has_pre, has_router = gpre is not None, router is not None
    in_specs, out_specs, out_shape = _epilogue_specs(tm, lambda i, k: i, which_gt, which_pre, has_pre,
                                                     has_router, h_dtype)
    return pl.pallas_call(
        functools.partial(_mm_post_kernel, nk=nk, has_pre=has_pre, has_router=has_router),
        grid=(N_TOK // tm, nk),
        in_specs=[pl.BlockSpec((tm, tk), lambda i, k: (i, k)),
                  pl.BlockSpec((tk, D), lambda i, k: (k, 0))] + in_specs,
        out_specs=out_specs, out_shape=out_shape,
        scratch_shapes=[pltpu.VMEM((tm, D), F32)],
        compiler_params=_cparams(("parallel", "arbitrary")),
        name=name,
    )(a, w, *_epilogue_args(x, gpost, mod_l, gpre, mod_l, router))


def _ret_kernel(lg_ref, q_ref, k_ref, v_ref, g_ref, gn_ref, *refs, seq, positional, has_init, write_state):
    it = iter(refs)
    cos_ref = sin_ref = s0_ref = sf_ref = None
    if positional:
        cos_ref, sin_ref = next(it), next(it)
    if has_init:
        s0_ref = next(it)
    y_ref = next(it)
    if write_state:
        sf_ref = next(it)
    qs, ks, yf, st = next(it), next(it), next(it), next(it)

    hd = pl.program_id(1)
    nchunk = seq // CHUNK
    lgf = lg_ref[0, hd]
    lgb = lg_ref[1, hd]

    def rot(x_ref_, c0):
        x = x_ref_[pl.ds(c0, CHUNK), :]
        if not positional:
            return x
        cs = cos_ref[pl.ds(c0, CHUNK), :]
        sn = sin_ref[pl.ds(c0, CHUNK), :]
        half = DK // 2
        parts = []
        for p in range(2):
            xp = x[:, p * half:(p + 1) * half]
            rolled = pltpu.roll(xp, half // 2, 1)
            parts.append(xp * cs[:, p * half:(p + 1) * half] + rolled * sn[:, p * half:(p + 1) * half])
        return jnp.concatenate(parts, axis=1)

    def prep(c, carry):
        c0 = pl.multiple_of(c * CHUNK, CHUNK)
        qs[pl.ds(c0, CHUNK), :] = rot(q_ref, c0)
        ks[pl.ds(c0, CHUNK), :] = rot(k_ref, c0) * (DK ** -0.5)
        return carry

    lax.fori_loop(0, nchunk, prep, 0)

    row = lax.broadcasted_iota(jnp.int32, (CHUNK, CHUNK), 0)
    col = lax.broadcasted_iota(jnp.int32, (CHUNK, CHUNK), 1)
    diff = (row - col).astype(F32)
    li = lax.broadcasted_iota(jnp.int32, (CHUNK, 1), 0).astype(F32)

    def run(direction, lg, body_out):
        if direction == 0:
            intra = jnp.where(diff >= 0, jnp.exp(lg * jnp.maximum(diff, 0.0)), 0.0)
            q_dec = jnp.exp(lg * (li + 1.0))
            k_dec = jnp.exp(lg * (CHUNK - 1.0 - li))
        else:
            intra = jnp.where(diff <= 0, jnp.exp(lg * jnp.maximum(-diff, 0.0)), 0.0)
            q_dec = jnp.exp(lg * (CHUNK - li))
            k_dec = jnp.exp(lg * li)
        chunk_dec = jnp.exp(jnp.full((1, 1), lg, F32) * CHUNK)
        if has_init:
            st[...] = s0_ref[direction]
        else:
            st[...] = jnp.zeros_like(st)

        def step(i, carry):
            c = i if direction == 0 else nchunk - 1 - i
            c0 = pl.multiple_of(c * CHUNK, CHUNK)
            qc = qs[pl.ds(c0, CHUNK), :]
            kc = ks[pl.ds(c0, CHUNK), :]
            vc = v_ref[pl.ds(c0, CHUNK), :].astype(BF16)
            state = st[...]
            scores = lax.dot_general(qc.astype(BF16), kc.astype(BF16), (((1,), (1,)), ((), ())),
                                     preferred_element_type=F32) * intra
            y = _dot(scores.astype(BF16), vc) + _dot((qc * q_dec).astype(BF16), state.astype(BF16))
            st[...] = state * chunk_dec + lax.dot_general(
                (kc * k_dec).astype(BF16), vc, (((0,), (0,)), ((), ())), preferred_element_type=F32)
            body_out(c0, y)
            return carry

        lax.fori_loop(0, nchunk, step, 0)
        if write_state:
            sf_ref[direction] = st[...]

    def out_fwd(c0, y):
        yf[pl.ds(c0, CHUNK), :] = y

    def out_bwd(c0, yb):
        y = yf[pl.ds(c0, CHUNK), :] + yb
        mu = jnp.mean(y, axis=-1, keepdims=True)
        yc = y - mu
        var = jnp.mean(yc * yc, axis=-1, keepdims=True)
        g = g_ref[pl.ds(c0, CHUNK), :]
        o = (g * _sigmoid(g)) * (yc * lax.rsqrt(var + EPS) * gn_ref[...])
        y_ref[pl.ds(c0, CHUNK), :] = o.astype(y_ref.dtype)

    run(0, lgf, out_fwd)
    run(1, lgb, out_bwd)


def _retention(proj, log_g, gn, rope, init_state, row_off, nb, seq, positional, write_state):
    rb = row_off // seq
    has_init = init_state is not None
    in_specs = [
        pl.BlockSpec(memory_space=pltpu.SMEM),
        pl.BlockSpec((seq, DK), lambda b, h: (rb + b, h)),
        pl.BlockSpec((seq, DK), lambda b, h: (rb + b, HEADS + h)),
        pl.BlockSpec((seq, DV), lambda b, h: (rb + b, HEADS + h)),
        pl.BlockSpec((seq, DV), lambda b, h: (rb + b, 2 * HEADS + h)),
        pl.BlockSpec((1, DV), lambda b, h: (0, h)),
    ]
    args = [log_g, proj, proj, proj, proj, gn]
    if positional:
        in_specs += [pl.BlockSpec((seq, DK), lambda b, h: (0, 0))] * 2
        args += list(rope)
    if has_init:
        in_specs.append(pl.BlockSpec((None, 2, None, DK, DV), lambda b, h: (b, 0, h, 0, 0)))
        args.append(init_state)
    out_specs = [pl.BlockSpec((seq, DV), lambda b, h: (b, h))]
    out_shape = [jax.ShapeDtypeStruct((nb * seq, VD), BF16)]
    if write_state:
        out_specs.append(pl.BlockSpec((None, 2, None, DK, DV), lambda b, h: (b, 0, h, 0, 0)))
        out_shape.append(jax.ShapeDtypeStruct((nb, 2, HEADS, DK, DV), F32))
    res = pl.pallas_call(
        functools.partial(_ret_kernel, seq=seq, positional=positional, has_init=has_init,
                          write_state=write_state),
        grid=(nb, HEADS),
        in_specs=in_specs, out_specs=out_specs, out_shape=out_shape,
        scratch_shapes=[pltpu.VMEM((seq, DK), F32), pltpu.VMEM((seq, DK), F32),
                        pltpu.VMEM((seq, DV), F32), pltpu.VMEM((DK, DV), F32)],
        compiler_params=_cparams(("parallel", "parallel")),
        name="retention_s%d" % seq,
    )(*args)
    return res


def _rope_tables(seq):
    half = DK // 4
    s = jnp.arange(seq, dtype=jnp.int32)
    freqs = ROPE_BASE ** (-jnp.arange(half, dtype=F32) / half)
    tabs_c, tabs_s = [], []
    for pos in ((s // GRID_W).astype(F32), (s % GRID_W).astype(F32)):
        ang = pos[:, None] * freqs[None, :]
        c, sn = jnp.cos(ang), jnp.sin(ang)
        tabs_c += [c, c]
        tabs_s += [-sn, sn]
    return jnp.concatenate(tabs_c, axis=1), jnp.concatenate(tabs_s, axis=1)


def _dft_mats(n):
    j = jnp.arange(n, dtype=jnp.int32)
    ang = ((j[:, None] * j[None, :]) % n).astype(F32) * (2.0 * np.pi / n)
    return jnp.cos(ang), jnp.sin(ang)


def _dft1_kernel(h_ref, w_ref, a_ref, b_ref):
    w = w_ref[...]
    for g in range(FNET_GROUPS):
        sl = slice(g * FNET_C, (g + 1) * FNET_C)
        r = _dot(h_ref[:, sl], w)
        a_ref[:, sl] = r[:, :FNET_C].astype(a_ref.dtype)
        b_ref[:, sl] = r[:, FNET_C:].astype(b_ref.dtype)


def _dft1(h, wcs):
    tm = 512
    return pl.pallas_call(
        _dft1_kernel,
        grid=(N_TOK // tm,),
        in_specs=[pl.BlockSpec((tm, D), lambda i: (i, 0)),
                  pl.BlockSpec((FNET_C, 2 * FNET_C), lambda i: (0, 0))],
        out_specs=[pl.BlockSpec((tm, D), lambda i: (i, 0))] * 2,
        out_shape=[jax.ShapeDtypeStruct((N_TOK, D), BF16)] * 2,
        compiler_params=_cparams(("parallel",)),
        name="dft_channels",
    )(h, wcs)


def _dft2_kernel(c_ref, s_ref, a_ref, b_ref, f_ref, *, scale):
    f = _dot(c_ref[...], a_ref[...]) - _dot(s_ref[...], b_ref[...])
    f_ref[...] = (f * scale).astype(f_ref.dtype)


def _dft2(a, b, cmat, smat, row_off, nb, seq):
    tr = 256
    nr = seq // tr
    sb = row_off // seq
    return pl.pallas_call(
        functools.partial(_dft2_kernel, scale=float(1.0 / np.sqrt(seq * FNET_C))),
        grid=(nb, nr),
        in_specs=[pl.BlockSpec((tr, seq), lambda bi, r: (r, 0)),
                  pl.BlockSpec((tr, seq), lambda bi, r: (r, 0)),
                  pl.BlockSpec((seq, D), lambda bi, r: (sb + bi, 0)),
                  pl.BlockSpec((seq, D), lambda bi, r: (sb + bi, 0))],
        out_specs=pl.BlockSpec((tr, D), lambda bi, r: (bi * nr + r, 0)),
        out_shape=jax.ShapeDtypeStruct((nb * seq, D), BF16),
        compiler_params=_cparams(("parallel", "parallel")),
        name="dft_positions_s%d" % seq,
    )(cmat, smat, a, b)


CONV_HALO = 16
CONV_PAD = CONV_WIDTH // 2


def _dwconv_kernel(prev_ref, cur_ref, next_ref, w_ref, b_ref, lg_ref, lb_ref, o_ref, win, cv, *, ts, nr):
    r = pl.program_id(1)
    zero = jnp.zeros((CONV_HALO, D), F32)
    win[0:CONV_HALO, :] = jnp.where(r > 0, prev_ref[...], zero)
    win[CONV_HALO:CONV_HALO + ts, :] = cur_ref[...]
    win[CONV_HALO + ts:, :] = jnp.where(r < nr - 1, next_ref[...], zero)
    lanes = 256
    rows = 128

    def col_body(cj, carry):
        c0 = pl.multiple_of(cj * lanes, lanes)
        for r0 in range(0, ts, rows):
            acc = jnp.broadcast_to(b_ref[:, pl.ds(c0, lanes)], (rows, lanes))
            for t in range(CONV_WIDTH):
                start = CONV_HALO - CONV_PAD + t + r0
                acc = acc + w_ref[t:t + 1, pl.ds(c0, lanes)] * win[start:start + rows, pl.ds(c0, lanes)]
            cv[r0:r0 + rows, pl.ds(c0, lanes)] = acc
        return carry

    lax.fori_loop(0, D // lanes, col_body, 0)
    u = cv[...]
    mu = jnp.mean(u, axis=-1, keepdims=True)
    uc = u - mu
    var = jnp.mean(uc * uc, axis=-1, keepdims=True)
    y = uc * lax.rsqrt(var + EPS) * lg_ref[...] + lb_ref[...]
    o_ref[...] = (y * _sigmoid(y)).astype(o_ref.dtype)


def _dwconv(u, wdw, bdw, ln_g, ln_b, row_off, nb, seq):
    ts = 256
    nr = seq // ts
    hb = ts // CONV_HALO
    tb = row_off // ts
    n_hblocks = N_TOK // CONV_HALO

    def cur(bi, r):
        return (tb + bi * nr + r, 0)

    def prev(bi, r):
        return (jnp.maximum((tb + bi * nr + r) * hb - 1, 0), 0)

    def nxt(bi, r):
        return (jnp.minimum((tb + bi * nr + r + 1) * hb, n_hblocks - 1), 0)

    const = lambda bi, r: (0, 0)
    return pl.pallas_call(
        functools.partial(_dwconv_kernel, ts=ts, nr=nr),
        grid=(nb, nr),
        in_specs=[pl.BlockSpec((CONV_HALO, D), prev), pl.BlockSpec((ts, D), cur),
                  pl.BlockSpec((CONV_HALO, D), nxt),
                  pl.BlockSpec((CONV_WIDTH + 1, D), const), pl.BlockSpec((1, D), const),
                  pl.BlockSpec((1, D), const), pl.BlockSpec((1, D), const)],
        out_specs=pl.BlockSpec((ts, D), lambda bi, r: (bi * nr + r, 0)),
        out_shape=jax.ShapeDtypeStruct((nb * seq, D), BF16),
        scratch_shapes=[pltpu.VMEM((ts + 2 * CONV_HALO, D), F32), pltpu.VMEM((ts, D), F32)],
        compiler_params=_cparams(("parallel", "parallel")),
        name="dwconv_s%d" % seq,
    )(u, u, u, wdw, bdw, ln_g, ln_b)


POOL_HALO = 8


def _pool_kernel(prev_ref, cur_ref, next_ref, w_ref, ps_ref, *refs, ts, seq, nr):
    win, yv = refs[-2], refs[-1]
    r = pl.program_id(1)
    win[0:POOL_HALO, :] = prev_ref[...]
    win[POOL_HALO:POOL_HALO + ts, :] = cur_ref[...]
    win[POOL_HALO + ts:, :] = next_ref[...]
    pos = r * ts + lax.broadcasted_iota(jnp.int32, (ts, 1), 0)
    for gi, wd in enumerate(POOL_WINDOWS):
        sl = slice(gi * POOL_GROUP, (gi + 1) * POOL_GROUP)
        half = wd // 2
        acc = jnp.zeros((ts, POOL_GROUP), F32)
        cnt = jnp.zeros((ts, 1), F32)
        for dlt in range(-half, half):
            ok = jnp.logical_and(pos + dlt >= 0, pos + dlt <= seq - 1)
            acc = acc + jnp.where(ok, win[POOL_HALO + dlt:POOL_HALO + dlt + ts, sl], 0.0)
            cnt = cnt + jnp.where(ok, 1.0, 0.0)
        pooled = acc / cnt - cur_ref[:, sl]
        yv[:, sl] = _dot(pooled.astype(BF16), w_ref[gi].astype(BF16))
    out = yv[...] * ps_ref[...]
    _epilogue_from_refs(out, refs[:-2], True, True)


def _pool(h, x, w_pool, pscale, gpost, mod_l, gpre, router, row_off, nb, seq):
    ts = 256
    nr = seq // ts
    hb = ts // POOL_HALO
    tb = row_off // ts
    n_hblocks = N_TOK // POOL_HALO

    def tile(bi, r):
        return tb + bi * nr + r

    def cur(bi, r):
        return (tile(bi, r), 0)

    def prev(bi, r):
        return (jnp.maximum(tile(bi, r) * hb - 1, 0), 0)

    def nxt(bi, r):
        return (jnp.minimum((tile(bi, r) + 1) * hb, n_hblocks - 1), 0)

    ep_in, ep_out, ep_shape = _epilogue_specs(ts, tile, 2, 3, True, True, BF16)
    ep_out = [pl.BlockSpec(s.block_shape, lambda bi, r: (bi * nr + r, 0)) for s in ep_out]
    ep_shape = [jax.ShapeDtypeStruct((nb * seq,) + s.shape[1:], s.dtype) for s in ep_shape]
    const = lambda bi, r: (0, 0)
    return pl.pallas_call(
        functools.partial(_pool_kernel, ts=ts, seq=seq, nr=nr),
        grid=(nb, nr),
        in_specs=[pl.BlockSpec((POOL_HALO, D), prev), pl.BlockSpec((ts, D), cur),
                  pl.BlockSpec((POOL_HALO, D), nxt),
                  pl.BlockSpec((len(POOL_WINDOWS), POOL_GROUP, POOL_GROUP), lambda bi, r: (0, 0, 0)),
                  pl.BlockSpec((1, D), const)] + ep_in,
        out_specs=ep_out, out_shape=ep_shape,
        scratch_shapes=[pltpu.VMEM((ts + 2 * POOL_HALO, D), F32), pltpu.VMEM((ts, D), F32)],
        compiler_params=_cparams(("parallel", "parallel")),
        name="pool_s%d" % seq,
    )(h, h, h, w_pool, pscale, *_epilogue_args(x, gpost, mod_l, gpre, mod_l, router))


FFN_ROWS = 512


def _ffn_up_kernel(xe_ref, wg_ref, wu_ref, hid_ref, wg_s, wu_s):
    wg_s[...] = wg_ref[...].astype(BF16)
    wu_s[...] = wu_ref[...].astype(BF16)

    def body(i, carry):
        r0 = pl.multiple_of(i * FFN_ROWS, FFN_ROWS)
        xe = xe_ref[pl.ds(r0, FFN_ROWS), :]
        a = _dot(xe, wg_s[...])
        b = _dot(xe, wu_s[...])
        hid_ref[pl.ds(r0, FFN_ROWS), :] = ((a * _sigmoid(a)) * b).astype(hid_ref.dtype)
        return carry

    lax.fori_loop(0, CAP_T // FFN_ROWS, body, 0)


def _ffn_down_kernel(hid_ref, wd_ref, gate_ref, ye_ref, wd_s):
    wd_s[...] = wd_ref[...].astype(BF16)

    def body(i, carry):
        r0 = pl.multiple_of(i * FFN_ROWS, FFN_ROWS)
        y = _dot(hid_ref[pl.ds(r0, FFN_ROWS), :], wd_s[...])
        ye_ref[pl.ds(r0, FFN_ROWS), :] = y * gate_ref[pl.ds(r0, FFN_ROWS), :]
        return carry

    lax.fori_loop(0, CAP_T // FFN_ROWS, body, 0)


def _expert_ffn(xe, gate, wg, wu, wd):
    tf, td = 512, 256
    hid = pl.pallas_call(
        _ffn_up_kernel,
        grid=(N_EXPERTS, D_FF // tf),
        in_specs=[pl.BlockSpec((None, CAP_T, D), lambda e, f: (e, 0, 0)),
                  pl.BlockSpec((None, D, tf), lambda e, f: (e, 0, f)),
                  pl.BlockSpec((None, D, tf), lambda e, f: (e, 0, f))],
        out_specs=pl.BlockSpec((None, CAP_T, tf), lambda e, f: (e, 0, f)),
        out_shape=jax.ShapeDtypeStruct((N_EXPERTS, CAP_T, D_FF), BF16),
        scratch_shapes=[pltpu.VMEM((D, tf), BF16), pltpu.VMEM((D, tf), BF16)],
        compiler_params=_cparams(("parallel", "parallel")),
        name="ffn_up",
    )(xe, wg, wu)
    return pl.pallas_call(
        _ffn_down_kernel,
        grid=(N_EXPERTS, D // td),
        in_specs=[pl.BlockSpec((None, CAP_T, D_FF), lambda e, j: (e, 0, 0)),
                  pl.BlockSpec((None, D_FF, td), lambda e, j: (e, 0, j)),
                  pl.BlockSpec((None, CAP_T, 1), lambda e, j: (e, 0, 0))],
        out_specs=pl.BlockSpec((None, CAP_T, td), lambda e, j: (e, 0, j)),
        out_shape=jax.ShapeDtypeStruct((N_EXPERTS, CAP_T, D), F32),
        scratch_shapes=[pltpu.VMEM((D_FF, td), BF16)],
        compiler_params=_cparams(("parallel", "parallel")),
        name="ffn_down",
    )(hid, wd, gate)


def _moe(h, aff, wg, wu, wd):
    gate_p, idx_p = lax.top_k(aff[:N_PROMPT, :N_EXPERTS].T, CAP_P)
    gate_s, idx_s = lax.top_k(aff[N_PROMPT:, :N_EXPERTS].T, CAP_S)
    idx = jnp.concatenate([idx_p, idx_s + N_PROMPT], axis=1)
    gate = jnp.concatenate([gate_p, gate_s], axis=1)[..., None]
    xe = h[idx]
    ye = _expert_ffn(xe, gate, wg, wu, wd)
    return jnp.zeros((N_TOK, D), F32).at[idx.reshape(-1)].add(ye.reshape(-1, D))


def kernel(x_prompt, x_sample, state_ret, c, c_ctx, w_mod, b_mod, g_norm, ret_w_in, ret_w_out, ret_decay,
           ret_gn, fnet_w, conv_w1, conv_wdw, conv_bdw, conv_ln_g, conv_ln_b, conv_w2, pool_w, pool_scale,
           moe_router, moe_w_gate, moe_w_up, moe_w_down):
    x = jnp.concatenate([x_prompt.reshape(N_PROMPT, D), x_sample.reshape(N_SAMPLE, D)], axis=0)
    cond_all = jnp.concatenate([c_ctx[None, :], c, jnp.zeros((MOD_ROWS - 1 - DEC_BATCH, D), F32)], axis=0)
    mod = _mod_table(cond_all, w_mod, b_mod)
    groups = ((0, BATCH, SEQ), (N_PROMPT, DEC_BATCH, DEC_SEQ))

    def gn(i, k):
        return g_norm[i, k][None, :]

    def router(i):
        wr = jnp.pad(moe_router[i], ((0, 0), (0, ROUTER_LANES - N_EXPERTS)))
        wrh = wr.astype(BF16)
        wrl = (wr - wrh.astype(F32)).astype(BF16)
        return wrh, wrl

    new_state = None
    h = _prenorm(x, gn(0, 0), mod[0], 0, BF16)
    for i in range(DEPTH):
        mod_l = mod[i]
        kind = i % 4
        if kind == 0:
            proj = _mm_ws(h, ret_w_in[0], IN_W, False, 512, 1024, F32, "ret_in_proj")
            log_g = jax.nn.log_sigmoid(ret_decay[0].astype(F32))
            ys = []
            for (row_off, nb, seq) in groups:
                positional = row_off != 0
                res = _retention(proj, log_g, ret_gn[0][None, :], _rope_tables(seq) if positional else None,
                                 state_ret[:, 0] if positional else None, row_off, nb, seq, positional,
                                 not positional)
                ys.append(res[0])
                if not positional:
                    new_state = res[1]
            a = jnp.concatenate(ys, axis=0)
            x, h, aff = _mm_post(a, ret_w_out[0], x, gn(i, 1), mod_l, 2, gn(i, 2), 3, router(i), BF16,
                                 "ret_out_proj")
        elif kind == 1:
            cc, sc = _dft_mats(FNET_C)
            wcs = jnp.concatenate([cc, sc], axis=1).astype(BF16)
            fa, fb = _dft1(h, wcs)
            fs = []
            for (row_off, nb, seq) in groups:
                cm, sm = _dft_mats(seq)
                fs.append(_dft2(fa, fb, cm.astype(BF16), sm.astype(BF16), row_off, nb, seq))
            a = jnp.concatenate(fs, axis=0)
            x, h, aff = _mm_post(a, fnet_w[0], x, gn(i, 1), mod_l, 2, gn(i, 2), 3, router(i), BF16,
                                 "fnet_proj")
        elif kind == 2:
            u = _mm_ws(h, conv_w1[0], D, True, 512, 512, F32, "conv_glu")
            wdw = jnp.pad(conv_wdw[0], ((0, 1), (0, 0)))
            vs = [_dwconv(u, wdw, conv_bdw[0][None, :], conv_ln_g[0][None, :], conv_ln_b[0][None, :],
                          row_off, nb, seq) for (row_off, nb, seq) in groups]
            a = jnp.concatenate(vs, axis=0)
            x, h, aff = _mm_post(a, conv_w2[0], x, gn(i, 1), mod_l, 2, gn(i, 2), 3, router(i), BF16,
                                 "conv_out_proj")
        else:
            outs = [_pool(h, x, pool_w[0], pool_scale[0][None, :], gn(i, 1), mod_l, gn(i, 2), router(i),
                          row_off, nb, seq) for (row_off, nb, seq) in groups]
            x = jnp.concatenate([o[0] for o in outs], axis=0)
            h = jnp.concatenate([o[1] for o in outs], axis=0)
            aff = jnp.concatenate([o[2] for o in outs], axis=0)
        y = _moe(h, aff, moe_w_gate[i], moe_w_up[i], moe_w_down[i])
        if i + 1 < DEPTH:
            nxt_dtype = F32 if (i + 1) % 4 == 3 else BF16
            x, h = _post(y, x, gn(i, 3), mod_l, 5, gn(i + 1, 0), mod[i + 1], 0, nxt_dtype)
        else:
            x, h = _post(y, x, gn(i, 3), mod_l, 5, None, None, 0, BF16)
    y_prompt = x[:N_PROMPT].reshape(BATCH, SEQ, D)
    y_sample = x[N_PROMPT:].reshape(DEC_BATCH, DEC_SEQ, D)
    return y_prompt, y_sample, new_state[:, None]
```

```python
import functools

import jax
import jax.numpy as jnp
import numpy as np
from jax import lax
from jax.experimental import pallas as pl
from jax.experimental.pallas import tpu as pltpu

F32 = jnp.float32
BF16 = jnp.bfloat16

D = 2048
BATCH, SEQ = 16, 256
DEC_BATCH, DEC_SEQ = 8, 1024
N_PROMPT = BATCH * SEQ
N_SAMPLE = DEC_BATCH * DEC_SEQ
N_TOK = N_PROMPT + N_SAMPLE
DEPTH = 4
GRID_W = 64
HEADS = 8
DK = D // HEADS
DV = 2 * DK
CHUNK = 128
ROPE_BASE = 10000.0
FNET_GROUPS = 4
FNET_C = D // FNET_GROUPS
CONV_WIDTH = 31
POOL_WINDOWS = (2, 4, 8, 16)
POOL_GROUP = D // len(POOL_WINDOWS)
N_EXPERTS = 16
CAP_P = 2 * N_PROMPT // N_EXPERTS
CAP_S = 2 * N_SAMPLE // N_EXPERTS
CAP_T = CAP_P + CAP_S
D_FF = 2 * D
N_MOD = 6
EPS = 1e-6
IN_W = 2 * HEADS * DK + 2 * HEADS * DV
VD = HEADS * DV
MOD_ROWS = 16
ROUTER_LANES = 128

VMEM_LIMIT = 56 * 1024 * 1024


def _cparams(sem):
    return pltpu.CompilerParams(dimension_semantics=sem, vmem_limit_bytes=VMEM_LIMIT)


def _sigmoid(x):
    return 1.0 / (1.0 + jnp.exp(-x))


def _dot(a, b):
    return jnp.dot(a, b, preferred_element_type=F32)


def _mod_row(row0):
    return jnp.where(row0 < N_PROMPT, 0, 1 + (row0 - N_PROMPT) // DEC_SEQ)


def _rms(v, g):
    ms = jnp.mean(v * v, axis=-1, keepdims=True)
    return v * lax.rsqrt(ms + EPS) * g


def _mod_kernel(c_ref, w_ref, b_ref, o_ref):
    c = c_ref[...]
    s = (c * _sigmoid(c)).astype(BF16)
    o_ref[...] = _dot(s, w_ref[...].astype(BF16)) + b_ref[...]


def _mod_table(cond_all, w_mod, b_mod):
    tn = 1024
    out = pl.pallas_call(
        _mod_kernel,
        grid=(DEPTH, N_MOD * D // tn),
        in_specs=[
            pl.BlockSpec((MOD_ROWS, D), lambda l, j: (0, 0)),
            pl.BlockSpec((None, D, tn), lambda l, j: (l, 0, j)),
            pl.BlockSpec((None, 1, tn), lambda l, j: (l, 0, j)),
        ],
        out_specs=pl.BlockSpec((None, MOD_ROWS, tn), lambda l, j: (l, 0, j)),
        out_shape=jax.ShapeDtypeStruct((DEPTH, MOD_ROWS, N_MOD * D), F32),
        compiler_params=_cparams(("parallel", "parallel")),
        name="mod_table",
    )(cond_all, w_mod, b_mod.reshape(DEPTH, 1, N_MOD * D))
    return out.reshape(DEPTH, MOD_ROWS * N_MOD, 1, D)


def _post_pre(out, x, gpost, gt, pre, router):
    xn = x + gt * _rms(out, gpost)
    if pre is None:
        return xn, None, None
    gpre, sc, sh = pre
    h = _rms(xn, gpre) * (1.0 + sc) + sh
    if router is None:
        return xn, h, None
    wrh, wrl = router
    hh = h.astype(BF16)
    hl = (h - hh.astype(F32)).astype(BF16)
    lg = _dot(hh, wrh) + _dot(hh, wrl) + _dot(hl, wrh)
    lane = lax.broadcasted_iota(jnp.int32, lg.shape, 1)
    lg = jnp.where(lane < N_EXPERTS, lg, -jnp.inf)
    e = jnp.exp(lg - jnp.max(lg, axis=-1, keepdims=True))
    aff = e / jnp.sum(e, axis=-1, keepdims=True)
    return xn, h, aff


def _epilogue_from_refs(out, refs, has_pre, has_router):
    it = iter(refs)
    x_ref, gpost_ref, gt_ref = next(it), next(it), next(it)
    pre = router = None
    if has_pre:
        pre = (next(it)[...], next(it)[...], next(it)[...])
    if has_router:
        router = (next(it)[...], next(it)[...])
    xo_ref = next(it)
    xn, h, aff = _post_pre(out, x_ref[...], gpost_ref[...], gt_ref[...], pre, router)
    xo_ref[...] = xn
    if has_pre:
        h_ref = next(it)
        h_ref[...] = h.astype(h_ref.dtype)
    if has_router:
        aff_ref = next(it)
        aff_ref[...] = aff


def _epilogue_specs(tm, row_imap, which_gt, which_pre, has_pre, has_router, h_dtype, n_rows=N_TOK,
                    mod_row0=None):
    if mod_row0 is None:
        mod_row0 = lambda *idx: row_imap(*idx) * tm

    def rows(*idx):
        return (row_imap(*idx), 0)

    def const(*idx):
        return (0, 0)

    def modmap(which):
        return lambda *idx: (_mod_row(mod_row0(*idx)) * N_MOD + which, 0, 0)

    in_specs = [pl.BlockSpec((tm, D), rows), pl.BlockSpec((1, D), const),
                pl.BlockSpec((None, 1, D), modmap(which_gt))]
    if has_pre:
        in_specs += [pl.BlockSpec((1, D), const),
                     pl.BlockSpec((None, 1, D), modmap(which_pre + 1)),
                     pl.BlockSpec((None, 1, D), modmap(which_pre))]
    if has_router:
        in_specs += [pl.BlockSpec((D, ROUTER_LANES), const), pl.BlockSpec((D, ROUTER_LANES), const)]
    out_specs = [pl.BlockSpec((tm, D), rows)]
    out_shape = [jax.ShapeDtypeStruct((n_rows, D), F32)]
    if has_pre:
        out_specs.append(pl.BlockSpec((tm, D), rows))
        out_shape.append(jax.ShapeDtypeStruct((n_rows, D), h_dtype))
    if has_router:
        out_specs.append(pl.BlockSpec((tm, ROUTER_LANES), rows))
        out_shape.append(jax.ShapeDtypeStruct((n_rows, ROUTER_LANES), F32))
    return in_specs, out_specs, out_shape


def _epilogue_args(x, gpost, mod_post, gpre, mod_pre, router):
    args = [x, gpost, mod_post]
    if gpre is not None:
        args += [gpre, mod_pre, mod_pre]
    if router is not None:
        args += list(router)
    return args


def _prenorm_kernel(x_ref, g_ref, sc_ref, sh_ref, h_ref):
    h = _rms(x_ref[...], g_ref[...]) * (1.0 + sc_ref[...]) + sh_ref[...]
    h_ref[...] = h.astype(h_ref.dtype)


def _prenorm(x, g, mod_l, which_pre, h_dtype):
    tm = 512
    modmap = lambda which: (lambda i: (_mod_row(i * tm) * N_MOD + which, 0, 0))
    return pl.pallas_call(
        _prenorm_kernel,
        grid=(N_TOK // tm,),
        in_specs=[pl.BlockSpec((tm, D), lambda i: (i, 0)), pl.BlockSpec((1, D), lambda i: (0, 0)),
                  pl.BlockSpec((None, 1, D), modmap(which_pre + 1)),
                  pl.BlockSpec((None, 1, D), modmap(which_pre))],
        out_specs=pl.BlockSpec((tm, D), lambda i: (i, 0)),
        out_shape=jax.ShapeDtypeStruct((N_TOK, D), h_dtype),
        compiler_params=_cparams(("parallel",)),
        name="prenorm",
    )(x, g, mod_l, mod_l)


def _mm_ws_kernel(h_ref, *refs, glu):
    if glu:
        wa_ref, wb_ref, o_ref, wa_s, wb_s = refs
    else:
        wa_ref, o_ref, wa_s = refs

    @pl.when(pl.program_id(1) == 0)
    def _():
        wa_s[...] = wa_ref[...].astype(BF16)
        if glu:
            wb_s[...] = wb_ref[...].astype(BF16)

    h = h_ref[...]
    a = _dot(h, wa_s[...])
    if glu:
        a = a * _sigmoid(_dot(h, wb_s[...]))
    o_ref[...] = a.astype(o_ref.dtype)


def _mm_ws(h, w, n_out, glu, tm, tn, out_dtype, name):
    k = h.shape[1]
    nj = n_out // tn
    in_specs = [pl.BlockSpec((tm, k), lambda j, i: (i, 0)), pl.BlockSpec((k, tn), lambda j, i: (0, j))]
    args = [h, w]
    scratch = [pltpu.VMEM((k, tn), BF16)]
    if glu:
        in_specs.append(pl.BlockSpec((k, tn), lambda j, i: (0, nj + j)))
        args.append(w)
        scratch.append(pltpu.VMEM((k, tn), BF16))
    return pl.pallas_call(
        functools.partial(_mm_ws_kernel, glu=glu),
        grid=(nj, h.shape[0] // tm),
        in_specs=in_specs,
        out_specs=pl.BlockSpec((tm, tn), lambda j, i: (i, j)),
        out_shape=jax.ShapeDtypeStruct((h.shape[0], n_out), out_dtype),
        scratch_shapes=scratch,
        compiler_params=_cparams(("parallel", "arbitrary")),
        name=name,
    )(*args)


def _mm_post_kernel(ap_ref, as_ref, w_ref, *refs, nk, tiles_p, has_pre, has_router):
    acc_ref = refs[-1]
    i = pl.program_id(0)
    k = pl.program_id(1)

    @pl.when(k == 0)
    def _():
        acc_ref[...] = jnp.zeros_like(acc_ref)

    @pl.when(i < tiles_p)
    def _():
        acc_ref[...] += _dot(ap_ref[...], w_ref[...])

    @pl.when(i >= tiles_p)
    def _():
        acc_ref[...] += _dot(as_ref[...], w_ref[...])

    @pl.when(k == nk - 1)
    def _():
        _epilogue_from_refs(acc_ref[...], refs[:-1], has_pre, has_router)


def _mm_post(a_p, a_s, w, x, gpost, mod_l, which_gt, gpre, which_pre, router, h_dtype, name):
    tm, tk = 512, 1024
    kdim = w.shape[0]
    nk = kdim // tk
    tiles_p = N_PROMPT // tm
    has_pre, has_router = gpre is not None, router is not None
    in_specs, out_specs, out_shape = _epilogue_specs(tm, lambda i, k: i, which_gt, which_pre, has_pre,
                                                     has_router, h_dtype)
    return pl.pallas_call(
        functools.partial(_mm_post_kernel, nk=nk, tiles_p=tiles_p, has_pre=has_pre, has_router=has_router),
        grid=(N_TOK // tm, nk),
        in_specs=[pl.BlockSpec((tm, tk), lambda i, k: (jnp.minimum(i, tiles_p - 1), k)),
                  pl.BlockSpec((tm, tk), lambda i, k: (jnp.maximum(i - tiles_p, 0), k)),
                  pl.BlockSpec((tk, D), lambda i, k: (k, 0))] + in_specs,
        out_specs=out_specs, out_shape=out_shape,
        scratch_shapes=[pltpu.VMEM((tm, D), F32)],
        compiler_params=_cparams(("parallel", "arbitrary")),
        name=name,
    )(a_p, a_s, w, *_epilogue_args(x, gpost, mod_l, gpre, mod_l, router))


def _ret_kernel(lg_ref, q_ref, k_ref, v_ref, g_ref, gn_ref, *refs, seq, positional, has_init, write_state):
    it = iter(refs)
    cos_ref = sin_ref = s0_ref = sf_ref = None
    if positional:
        cos_ref, sin_ref = next(it), next(it)
    if has_init:
        s0_ref = next(it)
    y_ref = next(it)
    if write_state:
        sf_ref = next(it)
    qs, ks, yf, yb, stf, stb = (next(it) for _ in range(6))

    hd = pl.program_id(1)
    nchunk = seq // CHUNK
    lgf = lg_ref[0, hd]
    lgb = lg_ref[1, hd]

    def rot(x_ref_, c0):
        x = x_ref_[pl.ds(c0, CHUNK), :]
        if not positional:
            return x
        cs = cos_ref[pl.ds(c0, CHUNK), :]
        sn = sin_ref[pl.ds(c0, CHUNK), :]
        half = DK // 2
        parts = []
        for p in range(2):
            xp = x[:, p * half:(p + 1) * half]
            rolled = pltpu.roll(xp, half // 2, 1)
            parts.append(xp * cs[:, p * half:(p + 1) * half] + rolled * sn[:, p * half:(p + 1) * half])
        return jnp.concatenate(parts, axis=1)

    def prep(c, carry):
        c0 = pl.multiple_of(c * CHUNK, CHUNK)
        qs[pl.ds(c0, CHUNK), :] = rot(q_ref, c0)
        ks[pl.ds(c0, CHUNK), :] = rot(k_ref, c0) * (DK ** -0.5)
        return carry

    lax.fori_loop(0, nchunk, prep, 0)

    row = lax.broadcasted_iota(jnp.int32, (CHUNK, CHUNK), 0)
    col = lax.broadcasted_iota(jnp.int32, (CHUNK, CHUNK), 1)
    diff = (row - col).astype(F32)
    li = lax.broadcasted_iota(jnp.int32, (CHUNK, 1), 0).astype(F32)

    def decays(direction, lg):
        if direction == 0:
            intra = jnp.where(diff >= 0, jnp.exp(lg * jnp.maximum(diff, 0.0)), 0.0)
            q_dec = jnp.exp(lg * (li + 1.0))
            k_dec = jnp.exp(lg * (CHUNK - 1.0 - li))
        else:
            intra = jnp.where(diff <= 0, jnp.exp(lg * jnp.maximum(-diff, 0.0)), 0.0)
            q_dec = jnp.exp(lg * (CHUNK - li))
            k_dec = jnp.exp(lg * li)
        return intra, q_dec, k_dec, jnp.exp(jnp.full((1, 1), lg, F32) * CHUNK)

    dirs = ((0, stf, yf, decays(0, lgf)), (1, stb, yb, decays(1, lgb)))
    for direction, st, _, _ in dirs:
        if has_init:
            st[...] = s0_ref[direction]
        else:
            st[...] = jnp.zeros_like(st)

    def step(c, st, yout, dec):
        intra, q_dec, k_dec, chunk_dec = dec
        rows = slice(c * CHUNK, (c + 1) * CHUNK)
        qc = qs[rows, :]
        kc = ks[rows, :]
        vc = v_ref[rows, :].astype(BF16)
        state = st[...]
        scores = lax.dot_general(qc.astype(BF16), kc.astype(BF16), (((1,), (1,)), ((), ())),
                                 preferred_element_type=F32) * intra
        yout[rows, :] = _dot(scores.astype(BF16), vc) + _dot((qc * q_dec).astype(BF16), state.astype(BF16))
        st[...] = state * chunk_dec + lax.dot_general(
            (kc * k_dec).astype(BF16), vc, (((0,), (0,)), ((), ())), preferred_element_type=F32)

    for i in range(nchunk):
        step(i, stf, yf, dirs[0][3])
        step(nchunk - 1 - i, stb, yb, dirs[1][3])
    if write_state:
        sf_ref[0] = stf[...]
        sf_ref[1] = stb[...]

    def finish(c, carry):
        c0 = pl.multiple_of(c * CHUNK, CHUNK)
        y = yf[pl.ds(c0, CHUNK), :] + yb[pl.ds(c0, CHUNK), :]
        mu = jnp.mean(y, axis=-1, keepdims=True)
        yc = y - mu
        var = jnp.mean(yc * yc, axis=-1, keepdims=True)
        g = g_ref[pl.ds(c0, CHUNK), :]
        o = (g * _sigmoid(g)) * (yc * lax.rsqrt(var + EPS) * gn_ref[...])
        y_ref[pl.ds(c0, CHUNK), :] = o.astype(y_ref.dtype)
        return carry

    lax.fori_loop(0, nchunk, finish, 0)


def _retention(proj, log_g, gn, rope, init_state, row_off, nb, seq, positional, write_state):
    rb = row_off // seq
    has_init = init_state is not None
    in_specs = [
        pl.BlockSpec(memory_space=pltpu.SMEM),
        pl.BlockSpec((seq, DK), lambda b, h: (rb + b, h)),
        pl.BlockSpec((seq, DK), lambda b, h: (rb + b, HEADS + h)),
        pl.BlockSpec((seq, DV), lambda b, h: (rb + b, HEADS + h)),
        pl.BlockSpec((seq, DV), lambda b, h: (rb + b, 2 * HEADS + h)),
        pl.BlockSpec((1, DV), lambda b, h: (0, h)),
    ]
    args = [log_g, proj, proj, proj, proj, gn]
    if positional:
        in_specs += [pl.BlockSpec((seq, DK), lambda b, h: (0, 0))] * 2
        args += list(rope)
    if has_init:
        in_specs.append(pl.BlockSpec((None, 2, None, DK, DV), lambda b, h: (b, 0, h, 0, 0)))
        args.append(init_state)
    out_specs = [pl.BlockSpec((seq, DV), lambda b, h: (b, h))]
    out_shape = [jax.ShapeDtypeStruct((nb * seq, VD), BF16)]
    if write_state:
        out_specs.append(pl.BlockSpec((None, 2, None, DK, DV), lambda b, h: (b, 0, h, 0, 0)))
        out_shape.append(jax.ShapeDtypeStruct((nb, 2, HEADS, DK, DV), F32))
    res = pl.pallas_call(
        functools.partial(_ret_kernel, seq=seq, positional=positional, has_init=has_init,
                          write_state=write_state),
        grid=(nb, HEADS),
        in_specs=in_specs, out_specs=out_specs, out_shape=out_shape,
        scratch_shapes=[pltpu.VMEM((seq, DK), F32), pltpu.VMEM((seq, DK), F32),
                        pltpu.VMEM((seq, DV), F32), pltpu.VMEM((seq, DV), F32),
                        pltpu.VMEM((DK, DV), F32), pltpu.VMEM((DK, DV), F32)],
        compiler_params=_cparams(("parallel", "parallel")),
        name="retention_s%d" % seq,
    )(*args)
    return res


def _rope_tables(seq):
    half = DK // 4
    s = jnp.arange(seq, dtype=jnp.int32)
    freqs = ROPE_BASE ** (-jnp.arange(half, dtype=F32) / half)
    tabs_c, tabs_s = [], []
    for pos in ((s // GRID_W).astype(F32), (s % GRID_W).astype(F32)):
        ang = pos[:, None] * freqs[None, :]
        c, sn = jnp.cos(ang), jnp.sin(ang)
        tabs_c += [c, c]
        tabs_s += [-sn, sn]
    return jnp.concatenate(tabs_c, axis=1), jnp.concatenate(tabs_s, axis=1)


def _dft_mats(n):
    j = jnp.arange(n, dtype=jnp.int32)
    ang = ((j[:, None] * j[None, :]) % n).astype(F32) * (2.0 * np.pi / n)
    return jnp.cos(ang), jnp.sin(ang)


def _dft1_kernel(h_ref, w_ref, a_ref, b_ref):
    w = w_ref[...]
    for g in range(FNET_GROUPS):
        sl = slice(g * FNET_C, (g + 1) * FNET_C)
        r = _dot(h_ref[:, sl], w)
        a_ref[:, sl] = r[:, :FNET_C].astype(a_ref.dtype)
        b_ref[:, sl] = r[:, FNET_C:].astype(b_ref.dtype)


def _dft1(h, wcs):
    tm = 512
    return pl.pallas_call(
        _dft1_kernel,
        grid=(N_TOK // tm,),
        in_specs=[pl.BlockSpec((tm, D), lambda i: (i, 0)),
                  pl.BlockSpec((FNET_C, 2 * FNET_C), lambda i: (0, 0))],
        out_specs=[pl.BlockSpec((tm, D), lambda i: (i, 0))] * 2,
        out_shape=[jax.ShapeDtypeStruct((N_TOK, D), BF16)] * 2,
        compiler_params=_cparams(("parallel",)),
        name="dft_channels",
    )(h, wcs)


def _dft2_kernel(c_ref, s_ref, a_ref, b_ref, f_ref, *, scale):
    f = _dot(c_ref[...], a_ref[...]) - _dot(s_ref[...], b_ref[...])
    f_ref[...] = (f * scale).astype(f_ref.dtype)


def _dft2(a, b, cmat, smat, row_off, nb, seq):
    tr = 256
    nr = seq // tr
    sb = row_off // seq
    return pl.pallas_call(
        functools.partial(_dft2_kernel, scale=float(1.0 / np.sqrt(seq * FNET_C))),
        grid=(nb, nr),
        in_specs=[pl.BlockSpec((tr, seq), lambda bi, r: (r, 0)),
                  pl.BlockSpec((tr, seq), lambda bi, r: (r, 0)),
                  pl.BlockSpec((seq, D), lambda bi, r: (sb + bi, 0)),
                  pl.BlockSpec((seq, D), lambda bi, r: (sb + bi, 0))],
        out_specs=pl.BlockSpec((tr, D), lambda bi, r: (bi * nr + r, 0)),
        out_shape=jax.ShapeDtypeStruct((nb * seq, D), BF16),
        compiler_params=_cparams(("parallel", "parallel")),
        name="dft_positions_s%d" % seq,
    )(cmat, smat, a, b)


CONV_HALO = 16
CONV_PAD = CONV_WIDTH // 2


def _dwconv_kernel(prev_ref, cur_ref, next_ref, w_ref, b_ref, lg_ref, lb_ref, o_ref, win, cv, *, ts, nr):
    r = pl.program_id(1)
    zero = jnp.zeros((CONV_HALO, D), F32)
    win[0:CONV_HALO, :] = jnp.where(r > 0, prev_ref[...], zero)
    win[CONV_HALO:CONV_HALO + ts, :] = cur_ref[...]
    win[CONV_HALO + ts:, :] = jnp.where(r < nr - 1, next_ref[...], zero)
    lanes = 256
    rows = 128

    def col_body(cj, carry):
        c0 = pl.multiple_of(cj * lanes, lanes)
        for r0 in range(0, ts, rows):
            acc = jnp.broadcast_to(b_ref[:, pl.ds(c0, lanes)], (rows, lanes))
            for t in range(CONV_WIDTH):
                start = CONV_HALO - CONV_PAD + t + r0
                acc = acc + w_ref[t:t + 1, pl.ds(c0, lanes)] * win[start:start + rows, pl.ds(c0, lanes)]
            cv[r0:r0 + rows, pl.ds(c0, lanes)] = acc
        return carry

    lax.fori_loop(0, D // lanes, col_body, 0)
    u = cv[...]
    mu = jnp.mean(u, axis=-1, keepdims=True)
    uc = u - mu
    var = jnp.mean(uc * uc, axis=-1, keepdims=True)
    y = uc * lax.rsqrt(var + EPS) * lg_ref[...] + lb_ref[...]
    o_ref[...] = (y * _sigmoid(y)).astype(o_ref.dtype)


def _dwconv(u, wdw, bdw, ln_g, ln_b, row_off, nb, seq):
    ts = 256
    nr = seq // ts
    hb = ts // CONV_HALO
    tb = row_off // ts
    n_hblocks = N_TOK // CONV_HALO

    def cur(bi, r):
        return (tb + bi * nr + r, 0)

    def prev(bi, r):
        return (jnp.maximum((tb + bi * nr + r) * hb - 1, 0), 0)

    def nxt(bi, r):
        return (jnp.minimum((tb + bi * nr + r + 1) * hb, n_hblocks - 1), 0)

    const = lambda bi, r: (0, 0)
    return pl.pallas_call(
        functools.partial(_dwconv_kernel, ts=ts, nr=nr),
        grid=(nb, nr),
        in_specs=[pl.BlockSpec((CONV_HALO, D), prev), pl.BlockSpec((ts, D), cur),
                  pl.BlockSpec((CONV_HALO, D), nxt),
                  pl.BlockSpec((CONV_WIDTH + 1, D), const), pl.BlockSpec((1, D), const),
                  pl.BlockSpec((1, D), const), pl.BlockSpec((1, D), const)],
        out_specs=pl.BlockSpec((ts, D), lambda bi, r: (bi * nr + r, 0)),
        out_shape=jax.ShapeDtypeStruct((nb * seq, D), BF16),
        scratch_shapes=[pltpu.VMEM((ts + 2 * CONV_HALO, D), F32), pltpu.VMEM((ts, D), F32)],
        compiler_params=_cparams(("parallel", "parallel")),
        name="dwconv_s%d" % seq,
    )(u, u, u, wdw, bdw, ln_g, ln_b)


POOL_HALO = 8


def _pool_kernel(prev_ref, cur_ref, next_ref, w_ref, ps_ref, *refs, ts, seq, nr):
    win, yv = refs[-2], refs[-1]
    r = pl.program_id(1)
    win[0:POOL_HALO, :] = prev_ref[...]
    win[POOL_HALO:POOL_HALO + ts, :] = cur_ref[...]
    win[POOL_HALO + ts:, :] = next_ref[...]
    pos = r * ts + lax.broadcasted_iota(jnp.int32, (ts, 1), 0)
    for gi, wd in enumerate(POOL_WINDOWS):
        sl = slice(gi * POOL_GROUP, (gi + 1) * POOL_GROUP)
        half = wd // 2
        acc = jnp.zeros((ts, POOL_GROUP), F32)
        cnt = jnp.zeros((ts, 1), F32)
        for dlt in range(-half, half):
            ok = jnp.logical_and(pos + dlt >= 0, pos + dlt <= seq - 1)
            acc = acc + jnp.where(ok, win[POOL_HALO + dlt:POOL_HALO + dlt + ts, sl], 0.0)
            cnt = cnt + jnp.where(ok, 1.0, 0.0)
        pooled = acc / cnt - cur_ref[:, sl]
        yv[:, sl] = _dot(pooled.astype(BF16), w_ref[gi].astype(BF16))
    out = yv[...] * ps_ref[...]
    _epilogue_from_refs(out, refs[:-2], True, True)


def _pool(h, x, w_pool, pscale, gpost, mod_l, gpre, router, row_off, nb, seq):
    ts = 256
    nr = seq // ts
    hb = ts // POOL_HALO
    tb = row_off // ts
    n_hblocks = N_TOK // POOL_HALO

    def tile(bi, r):
        return tb + bi * nr + r

    def cur(bi, r):
        return (tile(bi, r), 0)

    def prev(bi, r):
        return (jnp.maximum(tile(bi, r) * hb - 1, 0), 0)

    def nxt(bi, r):
        return (jnp.minimum((tile(bi, r) + 1) * hb, n_hblocks - 1), 0)

    ep_in, ep_out, ep_shape = _epilogue_specs(ts, tile, 2, 3, True, True, BF16)
    ep_out = [pl.BlockSpec(s.block_shape, lambda bi, r: (bi * nr + r, 0)) for s in ep_out]
    ep_shape = [jax.ShapeDtypeStruct((nb * seq,) + s.shape[1:], s.dtype) for s in ep_shape]
    const = lambda bi, r: (0, 0)
    return pl.pallas_call(
        functools.partial(_pool_kernel, ts=ts, seq=seq, nr=nr),
        grid=(nb, nr),
        in_specs=[pl.BlockSpec((POOL_HALO, D), prev), pl.BlockSpec((ts, D), cur),
                  pl.BlockSpec((POOL_HALO, D), nxt),
                  pl.BlockSpec((len(POOL_WINDOWS), POOL_GROUP, POOL_GROUP), lambda bi, r: (0, 0, 0)),
                  pl.BlockSpec((1, D), const)] + ep_in,
        out_specs=ep_out, out_shape=ep_shape,
        scratch_shapes=[pltpu.VMEM((ts + 2 * POOL_HALO, D), F32), pltpu.VMEM((ts, D), F32)],
        compiler_params=_cparams(("parallel", "parallel")),
        name="pool_s%d" % seq,
    )(h, h, h, w_pool, pscale, *_epilogue_args(x, gpost, mod_l, gpre, mod_l, router))


FFN_ROWS = 512


def _ffn_up_kernel(xe_ref, wg_ref, wu_ref, hid_ref, wg_s, wu_s):
    wg_s[...] = wg_ref[...].astype(BF16)
    wu_s[...] = wu_ref[...].astype(BF16)

    def body(i, carry):
        r0 = pl.multiple_of(i * FFN_ROWS, FFN_ROWS)
        xe = xe_ref[pl.ds(r0, FFN_ROWS), :]
        a = _dot(xe, wg_s[...])
        b = _dot(xe, wu_s[...])
        hid_ref[pl.ds(r0, FFN_ROWS), :] = ((a * _sigmoid(a)) * b).astype(hid_ref.dtype)
        return carry

    lax.fori_loop(0, CAP_T // FFN_ROWS, body, 0)


def _ffn_down_kernel(hid_ref, wd_ref, ye_ref, wd_s):
    wd_s[...] = wd_ref[...].astype(BF16)

    def body(i, carry):
        r0 = pl.multiple_of(i * FFN_ROWS, FFN_ROWS)
        y = _dot(hid_ref[pl.ds(r0, FFN_ROWS), :], wd_s[...])
        ye_ref[pl.ds(r0, FFN_ROWS), :] = y.astype(ye_ref.dtype)
        return carry

    lax.fori_loop(0, CAP_T // FFN_ROWS, body, 0)


def _expert_ffn(xe, layer, wg, wu, wd):
    tf, td = 512, 256
    hid = pl.pallas_call(
        _ffn_up_kernel,
        grid=(N_EXPERTS, D_FF // tf),
        in_specs=[pl.BlockSpec((None, CAP_T, D), lambda e, f: (e, 0, 0)),
                  pl.BlockSpec((None, None, D, tf), lambda e, f: (layer, e, 0, f)),
                  pl.BlockSpec((None, None, D, tf), lambda e, f: (layer, e, 0, f))],
        out_specs=pl.BlockSpec((None, CAP_T, tf), lambda e, f: (e, 0, f)),
        out_shape=jax.ShapeDtypeStruct((N_EXPERTS, CAP_T, D_FF), BF16),
        scratch_shapes=[pltpu.VMEM((D, tf), BF16), pltpu.VMEM((D, tf), BF16)],
        compiler_params=_cparams(("parallel", "parallel")),
        name="ffn_up",
    )(xe, wg, wu)
    return pl.pallas_call(
        _ffn_down_kernel,
        grid=(N_EXPERTS, D // td),
        in_specs=[pl.BlockSpec((None, CAP_T, D_FF), lambda e, j: (e, 0, 0)),
                  pl.BlockSpec((None, None, D_FF, td), lambda e, j: (layer, e, 0, j))],
        out_specs=pl.BlockSpec((None, CAP_T, td), lambda e, j: (e, 0, j)),
        out_shape=jax.ShapeDtypeStruct((N_EXPERTS, CAP_T, D), BF16),
        scratch_shapes=[pltpu.VMEM((D_FF, td), BF16)],
        compiler_params=_cparams(("parallel", "parallel")),
        name="ffn_down",
    )(hid, wd)


RB = 256
N_RB = N_TOK // RB
SLOT_BLOCKS = CAP_T // RB


def _route_kernel(aff_ref, slot_ref, base_ref, cnt_ref, key_s, *, n, cap, slot_off):
    nb = n // RB
    key_s[...] = pltpu.bitcast(aff_ref[...], jnp.int32)

    def bit_body(i, t):
        cand = t | jnp.left_shift(jnp.int32(1), 30 - i)
        cnt = jnp.sum((key_s[...] >= cand).astype(jnp.int32), axis=0, keepdims=True)
        return jnp.where(cnt >= cap, cand, t)

    t = lax.fori_loop(0, 31, bit_body, jnp.zeros((1, ROUTER_LANES), jnp.int32))
    c_gt = jnp.sum((key_s[...] > t).astype(jnp.int32), axis=0, keepdims=True)
    need = (cap - c_gt).astype(F32)
    lane_ok = lax.broadcasted_iota(jnp.int32, (1, ROUTER_LANES), 1) < N_EXPERTS
    tri = (lax.broadcasted_iota(jnp.int32, (RB, RB), 0) >
           lax.broadcasted_iota(jnp.int32, (RB, RB), 1)).astype(BF16)

    def blk(b, carry):
        eq_carry, sel_carry = carry
        r0 = pl.multiple_of(b * RB, RB)
        k = key_s[pl.ds(r0, RB), :]
        eq = k == t
        eq_rank = _dot(tri, eq.astype(BF16)) + eq_carry
        sel = jnp.logical_or(k > t, jnp.logical_and(eq, eq_rank < need))
        self_f = sel.astype(F32)
        pos = _dot(tri, sel.astype(BF16)) + sel_carry
        slot = jnp.where(jnp.logical_and(sel, lane_ok), pos.astype(jnp.int32) + slot_off, -1)
        slot_ref[pl.ds(r0, RB), :] = slot
        n_sel = jnp.sum(self_f, axis=0, keepdims=True)
        base_ref[pl.ds(b, 1), :] = sel_carry.astype(jnp.int32) + slot_off
        cnt_ref[pl.ds(b, 1), :] = n_sel.astype(jnp.int32)
        return eq_carry + jnp.sum(eq.astype(F32), axis=0, keepdims=True), sel_carry + n_sel

    zero = jnp.zeros((1, ROUTER_LANES), F32)
    lax.fori_loop(0, nb, blk, (zero, zero))


def _route(aff_g, cap, slot_off):
    n = aff_g.shape[0]
    return pl.pallas_call(
        functools.partial(_route_kernel, n=n, cap=cap, slot_off=slot_off),
        out_shape=[jax.ShapeDtypeStruct((n, ROUTER_LANES), jnp.int32),
                   jax.ShapeDtypeStruct((n // RB, ROUTER_LANES), jnp.int32),
                   jax.ShapeDtypeStruct((n // RB, ROUTER_LANES), jnp.int32)],
        scratch_shapes=[pltpu.VMEM((n, ROUTER_LANES), jnp.int32)],
        compiler_params=pltpu.CompilerParams(vmem_limit_bytes=VMEM_LIMIT),
        name="route_n%d" % n,
    )(aff_g)


def _expert_column(vals, e):
    lane = lax.broadcasted_iota(jnp.int32, vals.shape, 1)
    return jnp.sum(jnp.where(lane == e, vals, jnp.zeros_like(vals)), axis=1, keepdims=True)


GATHER_G = 2


def _gather_kernel(base_ref, cnt_ref, slot_ref, h_ref, xe_ref):
    eg, j = pl.program_id(0), pl.program_id(1)

    @pl.when(j == 0)
    def _():
        xe_ref[...] = jnp.zeros_like(xe_ref)

    sio = lax.broadcasted_iota(jnp.int32, (RB, RB), 0)
    for u in range(GATHER_G):
        e = eg * GATHER_G + u
        base = base_ref[e * N_RB + j]
        cnt = cnt_ref[e * N_RB + j]

        @pl.when(cnt > 0)
        def _():
            a = pl.multiple_of(base & (-RB), RB)
            rel = slot_ref[pl.ds(e, 1), :] - a

            def add(off):
                onehot = (rel - off == sio).astype(BF16)
                rows = _dot(onehot, h_ref[...])
                r0 = pl.multiple_of(a + off, RB)
                xe_ref[u, pl.ds(r0, RB), :] += rows.astype(xe_ref.dtype)

            add(0)

            @pl.when(base + cnt > a + RB)
            def _():
                add(RB)


def _gather(h, slot_t, base_tbl, cnt_tbl):
    return pl.pallas_call(
        _gather_kernel,
        grid_spec=pltpu.PrefetchScalarGridSpec(
            num_scalar_prefetch=2,
            grid=(N_EXPERTS // GATHER_G, N_RB),
            in_specs=[pl.BlockSpec((N_EXPERTS, RB), lambda eg, j, b, c: (0, j)),
                      pl.BlockSpec((RB, D), lambda eg, j, b, c: (j, 0))],
            out_specs=pl.BlockSpec((GATHER_G, CAP_T, D), lambda eg, j, b, c: (eg, 0, 0)),
        ),
        out_shape=jax.ShapeDtypeStruct((N_EXPERTS, CAP_T, D), BF16),
        compiler_params=_cparams(("parallel", "arbitrary")),
        name="moe_gather",
    )(base_tbl, cnt_tbl, slot_t, h)


COMBINE_TILE = 512
COMBINE_JJ = COMBINE_TILE // RB


def _combine_kernel(base_ref, cnt_ref, slot_ref, gate_ref, ye0_ref, ye1_ref, *refs, has_pre):
    y_acc = refs[-1]
    tile, e, jj = pl.program_id(0), pl.program_id(1), pl.program_id(2)
    j = tile * COMBINE_JJ + jj
    r0 = pl.multiple_of(jj * RB, RB)

    @pl.when(e == 0)
    def _():
        y_acc[pl.ds(r0, RB), :] = jnp.zeros((RB, D), F32)

    base = base_ref[e * N_RB + j]
    cnt = cnt_ref[e * N_RB + j]

    @pl.when(cnt > 0)
    def _():
        a = base & (-RB)
        rel = _expert_column(slot_ref[...], e) - a
        gate = _expert_column(gate_ref[...], e)
        sio = lax.broadcasted_iota(jnp.int32, (RB, RB), 1)
        y_acc[pl.ds(r0, RB), :] += gate * _dot((rel == sio).astype(BF16), ye0_ref[...])

        @pl.when(base + cnt > a + RB)
        def _():
            y_acc[pl.ds(r0, RB), :] += gate * _dot((rel - RB == sio).astype(BF16), ye1_ref[...])

    @pl.when(jnp.logical_and(e == N_EXPERTS - 1, jj == COMBINE_JJ - 1))
    def _():
        _epilogue_from_refs(y_acc[...], refs[:-1], has_pre, False)


def _combine_post(ye, slot, aff, base_tbl, cnt_tbl, x, gpost, mod_l, gpre, mod_pre, h_dtype):
    def blk0(t, e, jj, b, c):
        return jnp.minimum(lax.shift_right_logical(b[e * N_RB + t * COMBINE_JJ + jj], 8), SLOT_BLOCKS - 1)

    def tok(t, e, jj, b, c):
        return (t * COMBINE_JJ + jj, 0)

    has_pre = gpre is not None
    ep_in, ep_out, ep_shape = _epilogue_specs(COMBINE_TILE, lambda t, e, jj, b, c: t, 5, 0, has_pre, False,
                                              h_dtype)
    res = pl.pallas_call(
        functools.partial(_combine_kernel, has_pre=has_pre),
        grid_spec=pltpu.PrefetchScalarGridSpec(
            num_scalar_prefetch=2,
            grid=(N_TOK // COMBINE_TILE, N_EXPERTS, COMBINE_JJ),
            in_specs=[pl.BlockSpec((RB, ROUTER_LANES), tok),
                      pl.BlockSpec((RB, ROUTER_LANES), tok),
                      pl.BlockSpec((None, RB, D), lambda t, e, jj, b, c: (e, blk0(t, e, jj, b, c), 0)),
                      pl.BlockSpec((None, RB, D), lambda t, e, jj, b, c: (
                          e, jnp.minimum(blk0(t, e, jj, b, c) + 1, SLOT_BLOCKS - 1), 0))] + ep_in,
            out_specs=ep_out,
            scratch_shapes=[pltpu.VMEM((COMBINE_TILE, D), F32)],
        ),
        out_shape=ep_shape,
        compiler_params=_cparams(("parallel", "arbitrary", "arbitrary")),
        name="moe_combine",
    )(base_tbl, cnt_tbl, slot, aff, ye, ye, *_epilogue_args(x, gpost, mod_l, gpre, mod_pre, None))
    return res if has_pre else (res[0], None)


def _moe(h, aff, layer, wg, wu, wd, x, gpost, mod_l, gpre, mod_pre, h_dtype):
    slot_p, base_p, cnt_p = _route(aff[:N_PROMPT], CAP_P, 0)
    slot_s, base_s, cnt_s = _route(aff[N_PROMPT:], CAP_S, CAP_P)
    slot = jnp.concatenate([slot_p, slot_s], axis=0)

    def table(tp, ts):
        return jnp.concatenate([tp, ts], axis=0)[:, :N_EXPERTS].T.reshape(-1)

    base_tbl, cnt_tbl = table(base_p, base_s), table(cnt_p, cnt_s)
    xe = _gather(h, slot[:, :N_EXPERTS].T, base_tbl, cnt_tbl)
    ye = _expert_ffn(xe, layer, wg, wu, wd)
    return _combine_post(ye, slot, aff, base_tbl, cnt_tbl, x, gpost, mod_l, gpre, mod_pre, h_dtype)


def kernel(x_prompt, x_sample, state_ret, c, c_ctx, w_mod, b_mod, g_norm, ret_w_in, ret_w_out, ret_decay,
           ret_gn, fnet_w, conv_w1, conv_wdw, conv_bdw, conv_ln_g, conv_ln_b, conv_w2, pool_w, pool_scale,
           moe_router, moe_w_gate, moe_w_up, moe_w_down):
    x = jnp.concatenate([x_prompt.reshape(N_PROMPT, D), x_sample.reshape(N_SAMPLE, D)], axis=0)
    cond_all = jnp.concatenate([c_ctx[None, :], c, jnp.zeros((MOD_ROWS - 1 - DEC_BATCH, D), F32)], axis=0)
    mod = _mod_table(cond_all, w_mod, b_mod)
    groups = ((0, BATCH, SEQ), (N_PROMPT, DEC_BATCH, DEC_SEQ))

    def gn(i, k):
        return g_norm[i, k][None, :]

    def router(i):
        wr = jnp.pad(moe_router[i], ((0, 0), (0, ROUTER_LANES - N_EXPERTS)))
        wrh = wr.astype(BF16)
        wrl = (wr - wrh.astype(F32)).astype(BF16)
        return wrh, wrl

    new_state = None
    h = _prenorm(x, gn(0, 0), mod[0], 0, BF16)
    for i in range(DEPTH):
        mod_l = mod[i]
        kind = i % 4
        if kind == 0:
            proj = _mm_ws(h, ret_w_in[0], IN_W, False, 512, 1024, F32, "ret_in_proj")
            log_g = jax.nn.log_sigmoid(ret_decay[0].astype(F32))
            ys = []
            for (row_off, nb, seq) in groups:
                positional = row_off != 0
                res = _retention(proj, log_g, ret_gn[0][None, :], _rope_tables(seq) if positional else None,
                                 state_ret[:, 0] if positional else None, row_off, nb, seq, positional,
                                 not positional)
                ys.append(res[0])
                if not positional:
                    new_state = res[1]
            x, h, aff = _mm_post(ys[0], ys[1], ret_w_out[0].astype(BF16), x, gn(i, 1), mod_l, 2, gn(i, 2), 3,
                                 router(i), BF16, "ret_out_proj")
        elif kind == 1:
            cc, sc = _dft_mats(FNET_C)
            wcs = jnp.concatenate([cc, sc], axis=1).astype(BF16)
            fa, fb = _dft1(h, wcs)
            fs = []
            for (row_off, nb, seq) in groups:
                cm, sm = _dft_mats(seq)
                fs.append(_dft2(fa, fb, cm.astype(BF16), sm.astype(BF16), row_off, nb, seq))
            x, h, aff = _mm_post(fs[0], fs[1], fnet_w[0].astype(BF16), x, gn(i, 1), mod_l, 2, gn(i, 2), 3,
                                 router(i), BF16, "fnet_proj")
        elif kind == 2:
            u = _mm_ws(h, conv_w1[0], D, True, 512, 512, F32, "conv_glu")
            wdw = jnp.pad(conv_wdw[0], ((0, 1), (0, 0)))
            vs = [_dwconv(u, wdw, conv_bdw[0][None, :], conv_ln_g[0][None, :], conv_ln_b[0][None, :],
                          row_off, nb, seq) for (row_off, nb, seq) in groups]
            x, h, aff = _mm_post(vs[0], vs[1], conv_w2[0].astype(BF16), x, gn(i, 1), mod_l, 2, gn(i, 2), 3,
                                 router(i), BF16, "conv_out_proj")
        else:
            outs = [_pool(h, x, pool_w[0], pool_scale[0][None, :], gn(i, 1), mod_l, gn(i, 2), router(i),
                          row_off, nb, seq) for (row_off, nb, seq) in groups]
            x = jnp.concatenate([o[0] for o in outs], axis=0)
            h = jnp.concatenate([o[1] for o in outs], axis=0)
            aff = jnp.concatenate([o[2] for o in outs], axis=0)
        if i + 1 < DEPTH:
            nxt_dtype = F32 if (i + 1) % 4 == 3 else BF16
            x, h = _moe(h, aff, i, moe_w_gate, moe_w_up, moe_w_down, x, gn(i, 3), mod_l, gn(i + 1, 0),
                        mod[i + 1], nxt_dtype)
        else:
            x, h = _moe(h, aff, i, moe_w_gate, moe_w_up, moe_w_down, x, gn(i, 3), mod_l, None, None, BF16)
    y_prompt = x[:N_PROMPT].reshape(BATCH, SEQ, D)
    y_sample = x[N_PROMPT:].reshape(DEC_BATCH, DEC_SEQ, D)
    return y_prompt, y_sample, new_state[:, None]
```

```python
import functools

import jax
import jax.numpy as jnp
import numpy as np
from jax import lax
from jax.experimental import pallas as pl
from jax.experimental.pallas import tpu as pltpu

F32 = jnp.float32
BF16 = jnp.bfloat16

D = 2048
BATCH, SEQ = 16, 256
DEC_BATCH, DEC_SEQ = 8, 1024
N_PROMPT = BATCH * SEQ
N_SAMPLE = DEC_BATCH * DEC_SEQ
N_TOK = N_PROMPT + N_SAMPLE
DEPTH = 4
GRID_W = 64
HEADS = 8
DK = D // HEADS
DV = 2 * DK
CHUNK = 128
ROPE_BASE = 10000.0
FNET_GROUPS = 4
FNET_C = D // FNET_GROUPS
CONV_WIDTH = 31
POOL_WINDOWS = (2, 4, 8, 16)
POOL_GROUP = D // len(POOL_WINDOWS)
N_EXPERTS = 16
CAP_P = 2 * N_PROMPT // N_EXPERTS
CAP_S = 2 * N_SAMPLE // N_EXPERTS
CAP_T = CAP_P + CAP_S
D_FF = 2 * D
N_MOD = 6
EPS = 1e-6
IN_W = 2 * HEADS * DK + 2 * HEADS * DV
VD = HEADS * DV
MOD_ROWS = 16
ROUTER_LANES = 128

VMEM_LIMIT = 56 * 1024 * 1024


def _cparams(sem):
    return pltpu.CompilerParams(dimension_semantics=sem, vmem_limit_bytes=VMEM_LIMIT)


def _sigmoid(x):
    return 1.0 / (1.0 + jnp.exp(-x))


def _dot(a, b):
    return jnp.dot(a, b, preferred_element_type=F32)


def _mod_row(row0):
    return jnp.where(row0 < N_PROMPT, 0, 1 + (row0 - N_PROMPT) // DEC_SEQ)


def _rms(v, g):
    ms = jnp.mean(v * v, axis=-1, keepdims=True)
    return v * lax.rsqrt(ms + EPS) * g


def _mod_kernel(c_ref, w_ref, b_ref, o_ref):
    c = c_ref[...]
    s = (c * _sigmoid(c)).astype(BF16)
    o_ref[...] = _dot(s, w_ref[...].astype(BF16)) + b_ref[...]


def _mod_table(cond_all, w_mod, b_mod):
    tn = 1024
    out = pl.pallas_call(
        _mod_kernel,
        grid=(DEPTH, N_MOD * D // tn),
        in_specs=[
            pl.BlockSpec((MOD_ROWS, D), lambda l, j: (0, 0)),
            pl.BlockSpec((None, D, tn), lambda l, j: (l, 0, j)),
            pl.BlockSpec((None, 1, tn), lambda l, j: (l, 0, j)),
        ],
        out_specs=pl.BlockSpec((None, MOD_ROWS, tn), lambda l, j: (l, 0, j)),
        out_shape=jax.ShapeDtypeStruct((DEPTH, MOD_ROWS, N_MOD * D), F32),
        compiler_params=_cparams(("parallel", "parallel")),
        name="mod_table",
    )(cond_all, w_mod, b_mod.reshape(DEPTH, 1, N_MOD * D))
    return out.reshape(DEPTH, MOD_ROWS * N_MOD, 1, D)


def _post_pre(out, x, gpost, gt, pre, router):
    xn = x + gt * _rms(out, gpost)
    if pre is None:
        return xn, None, None
    gpre, sc, sh = pre
    h = _rms(xn, gpre) * (1.0 + sc) + sh
    if router is None:
        return xn, h, None
    wrh, wrl = router
    hh = h.astype(BF16)
    hl = (h - hh.astype(F32)).astype(BF16)
    lg = _dot(hh, wrh) + _dot(hh, wrl) + _dot(hl, wrh)
    lane = lax.broadcasted_iota(jnp.int32, lg.shape, 1)
    lg = jnp.where(lane < N_EXPERTS, lg, -jnp.inf)
    e = jnp.exp(lg - jnp.max(lg, axis=-1, keepdims=True))
    aff = e / jnp.sum(e, axis=-1, keepdims=True)
    return xn, h, aff


def _epilogue_from_refs(out, refs, has_pre, has_router):
    it = iter(refs)
    x_ref, gpost_ref, gt_ref = next(it), next(it), next(it)
    pre = router = None
    if has_pre:
        pre = (next(it)[...], next(it)[...], next(it)[...])
    if has_router:
        router = (next(it)[...], next(it)[...])
    xo_ref = next(it)
    xn, h, aff = _post_pre(out, x_ref[...], gpost_ref[...], gt_ref[...], pre, router)
    xo_ref[...] = xn
    if has_pre:
        h_ref = next(it)
        h_ref[...] = h.astype(h_ref.dtype)
    if has_router:
        aff_ref = next(it)
        aff_ref[...] = aff


def _epilogue_specs(tm, row_imap, which_gt, which_pre, has_pre, has_router, h_dtype, n_rows=N_TOK,
                    mod_row0=None):
    if mod_row0 is None:
        mod_row0 = lambda *idx: row_imap(*idx) * tm

    def rows(*idx):
        return (row_imap(*idx), 0)

    def const(*idx):
        return (0, 0)

    def modmap(which):
        return lambda *idx: (_mod_row(mod_row0(*idx)) * N_MOD + which, 0, 0)

    in_specs = [pl.BlockSpec((tm, D), rows), pl.BlockSpec((1, D), const),
                pl.BlockSpec((None, 1, D), modmap(which_gt))]
    if has_pre:
        in_specs += [pl.BlockSpec((1, D), const),
                     pl.BlockSpec((None, 1, D), modmap(which_pre + 1)),
                     pl.BlockSpec((None, 1, D), modmap(which_pre))]
    if has_router:
        in_specs += [pl.BlockSpec((D, ROUTER_LANES), const), pl.BlockSpec((D, ROUTER_LANES), const)]
    out_specs = [pl.BlockSpec((tm, D), rows)]
    out_shape = [jax.ShapeDtypeStruct((n_rows, D), F32)]
    if has_pre:
        out_specs.append(pl.BlockSpec((tm, D), rows))
        out_shape.append(jax.ShapeDtypeStruct((n_rows, D), h_dtype))
    if has_router:
        out_specs.append(pl.BlockSpec((tm, ROUTER_LANES), rows))
        out_shape.append(jax.ShapeDtypeStruct((n_rows, ROUTER_LANES), F32))
    return in_specs, out_specs, out_shape


def _epilogue_args(x, gpost, mod_post, gpre, mod_pre, router):
    args = [x, gpost, mod_post]
    if gpre is not None:
        args += [gpre, mod_pre, mod_pre]
    if router is not None:
        args += list(router)
    return args


def _prenorm_kernel(x_ref, g_ref, sc_ref, sh_ref, h_ref):
    h = _rms(x_ref[...], g_ref[...]) * (1.0 + sc_ref[...]) + sh_ref[...]
    h_ref[...] = h.astype(h_ref.dtype)


def _prenorm(x, g, mod_l, which_pre, h_dtype):
    tm = 512
    modmap = lambda which: (lambda i: (_mod_row(i * tm) * N_MOD + which, 0, 0))
    return pl.pallas_call(
        _prenorm_kernel,
        grid=(N_TOK // tm,),
        in_specs=[pl.BlockSpec((tm, D), lambda i: (i, 0)), pl.BlockSpec((1, D), lambda i: (0, 0)),
                  pl.BlockSpec((None, 1, D), modmap(which_pre + 1)),
                  pl.BlockSpec((None, 1, D), modmap(which_pre))],
        out_specs=pl.BlockSpec((tm, D), lambda i: (i, 0)),
        out_shape=jax.ShapeDtypeStruct((N_TOK, D), h_dtype),
        compiler_params=_cparams(("parallel",)),
        name="prenorm",
    )(x, g, mod_l, mod_l)


def _mm_ws_kernel(h_ref, *refs, glu):
    if glu:
        wa_ref, wb_ref, o_ref, wa_s, wb_s = refs
    else:
        wa_ref, o_ref, wa_s = refs

    @pl.when(pl.program_id(1) == 0)
    def _():
        wa_s[...] = wa_ref[...].astype(BF16)
        if glu:
            wb_s[...] = wb_ref[...].astype(BF16)

    h = h_ref[...]
    a = _dot(h, wa_s[...])
    if glu:
        a = a * _sigmoid(_dot(h, wb_s[...]))
    o_ref[...] = a.astype(o_ref.dtype)


def _mm_ws(h, w, n_out, glu, tm, tn, out_dtype, name):
    k = h.shape[1]
    nj = n_out // tn
    in_specs = [pl.BlockSpec((tm, k), lambda j, i: (i, 0)), pl.BlockSpec((k, tn), lambda j, i: (0, j))]
    args = [h, w]
    scratch = [pltpu.VMEM((k, tn), BF16)]
    if glu:
        in_specs.append(pl.BlockSpec((k, tn), lambda j, i: (0, nj + j)))
        args.append(w)
        scratch.append(pltpu.VMEM((k, tn), BF16))
    return pl.pallas_call(
        functools.partial(_mm_ws_kernel, glu=glu),
        grid=(nj, h.shape[0] // tm),
        in_specs=in_specs,
        out_specs=pl.BlockSpec((tm, tn), lambda j, i: (i, j)),
        out_shape=jax.ShapeDtypeStruct((h.shape[0], n_out), out_dtype),
        scratch_shapes=scratch,
        compiler_params=_cparams(("parallel", "arbitrary")),
        name=name,
    )(*args)


def _mm_post_kernel(ap_ref, as_ref, w_ref, *refs, nk, tiles_p, has_pre, has_router):
    acc_ref = refs[-1]
    i = pl.program_id(0)
    k = pl.program_id(1)

    @pl.when(k == 0)
    def _():
        acc_ref[...] = jnp.zeros_like(acc_ref)

    @pl.when(i < tiles_p)
    def _():
        acc_ref[...] += _dot(ap_ref[...], w_ref[...])

    @pl.when(i >= tiles_p)
    def _():
        acc_ref[...] += _dot(as_ref[...], w_ref[...])

    @pl.when(k == nk - 1)
    def _():
        _epilogue_from_refs(acc_ref[...], refs[:-1], has_pre, has_router)


def _mm_post(a_p, a_s, w, x, gpost, mod_l, which_gt, gpre, which_pre, router, h_dtype, name):
    tm, tk = 512, 1024
    kdim = w.shape[0]
    nk = kdim // tk
    tiles_p = N_PROMPT // tm
    has_pre, has_router = gpre is not None, router is not None
    in_specs, out_specs, out_shape = _epilogue_specs(tm, lambda i, k: i, which_gt, which_pre, has_pre,
                                                     has_router, h_dtype)
    return pl.pallas_call(
        functools.partial(_mm_post_kernel, nk=nk, tiles_p=tiles_p, has_pre=has_pre, has_router=has_router),
        grid=(N_TOK // tm, nk),
        in_specs=[pl.BlockSpec((tm, tk), lambda i, k: (jnp.minimum(i, tiles_p - 1), k)),
                  pl.BlockSpec((tm, tk), lambda i, k: (jnp.maximum(i - tiles_p, 0), k)),
                  pl.BlockSpec((tk, D), lambda i, k: (k, 0))] + in_specs,
        out_specs=out_specs, out_shape=out_shape,
        scratch_shapes=[pltpu.VMEM((tm, D), F32)],
        compiler_params=_cparams(("parallel", "arbitrary")),
        name=name,
    )(a_p, a_s, w, *_epilogue_args(x, gpost, mod_l, gpre, mod_l, router))


def _ret_kernel(lg_ref, q_ref, k_ref, v_ref, g_ref, gn_ref, *refs, seq, positional, has_init, write_state):
    it = iter(refs)
    cos_ref = sin_ref = s0_ref = sf_ref = None
    if positional:
        cos_ref, sin_ref = next(it), next(it)
    if has_init:
        s0_ref = next(it)
    y_ref = next(it)
    if write_state:
        sf_ref = next(it)
    qs, ks, yf, yb, stf, stb = (next(it) for _ in range(6))

    hd = pl.program_id(1)
    nchunk = seq // CHUNK
    lgf = lg_ref[0, hd]
    lgb = lg_ref[1, hd]

    def rot(x_ref_, c0):
        x = x_ref_[pl.ds(c0, CHUNK), :]
        if not positional:
            return x
        cs = cos_ref[pl.ds(c0, CHUNK), :]
        sn = sin_ref[pl.ds(c0, CHUNK), :]
        half = DK // 2
        parts = []
        for p in range(2):
            xp = x[:, p * half:(p + 1) * half]
            rolled = pltpu.roll(xp, half // 2, 1)
            parts.append(xp * cs[:, p * half:(p + 1) * half] + rolled * sn[:, p * half:(p + 1) * half])
        return jnp.concatenate(parts, axis=1)

    def prep(c, carry):
        c0 = pl.multiple_of(c * CHUNK, CHUNK)
        qs[pl.ds(c0, CHUNK), :] = rot(q_ref, c0)
        ks[pl.ds(c0, CHUNK), :] = rot(k_ref, c0) * (DK ** -0.5)
        return carry

    lax.fori_loop(0, nchunk, prep, 0)

    row = lax.broadcasted_iota(jnp.int32, (CHUNK, CHUNK), 0)
    col = lax.broadcasted_iota(jnp.int32, (CHUNK, CHUNK), 1)
    diff = (row - col).astype(F32)
    li = lax.broadcasted_iota(jnp.int32, (CHUNK, 1), 0).astype(F32)

    def decays(direction, lg):
        if direction == 0:
            intra = jnp.where(diff >= 0, jnp.exp(lg * jnp.maximum(diff, 0.0)), 0.0)
            q_dec = jnp.exp(lg * (li + 1.0))
            k_dec = jnp.exp(lg * (CHUNK - 1.0 - li))
        else:
            intra = jnp.where(diff <= 0, jnp.exp(lg * jnp.maximum(-diff, 0.0)), 0.0)
            q_dec = jnp.exp(lg * (CHUNK - li))
            k_dec = jnp.exp(lg * li)
        return intra, q_dec, k_dec, jnp.exp(jnp.full((1, 1), lg, F32) * CHUNK)

    dirs = ((0, stf, yf, decays(0, lgf)), (1, stb, yb, decays(1, lgb)))
    for direction, st, _, _ in dirs:
        if has_init:
            st[...] = s0_ref[direction]
        else:
            st[...] = jnp.zeros_like(st)

    def step(c, st, yout, dec):
        intra, q_dec, k_dec, chunk_dec = dec
        rows = slice(c * CHUNK, (c + 1) * CHUNK)
        qc = qs[rows, :]
        kc = ks[rows, :]
        vc = v_ref[rows, :].astype(BF16)
        state = st[...]
        scores = lax.dot_general(qc.astype(BF16), kc.astype(BF16), (((1,), (1,)), ((), ())),
                                 preferred_element_type=F32) * intra
        yout[rows, :] = _dot(scores.astype(BF16), vc) + _dot((qc * q_dec).astype(BF16), state.astype(BF16))
        st[...] = state * chunk_dec + lax.dot_general(
            (kc * k_dec).astype(BF16), vc, (((0,), (0,)), ((), ())), preferred_element_type=F32)

    for i in range(nchunk):
        step(i, stf, yf, dirs[0][3])
        step(nchunk - 1 - i, stb, yb, dirs[1][3])
    if write_state:
        sf_ref[0] = stf[...]
        sf_ref[1] = stb[...]

    def finish(c, carry):
        c0 = pl.multiple_of(c * CHUNK, CHUNK)
        y = yf[pl.ds(c0, CHUNK), :] + yb[pl.ds(c0, CHUNK), :]
        mu = jnp.mean(y, axis=-1, keepdims=True)
        yc = y - mu
        var = jnp.mean(yc * yc, axis=-1, keepdims=True)
        g = g_ref[pl.ds(c0, CHUNK), :]
        o = (g * _sigmoid(g)) * (yc * lax.rsqrt(var + EPS) * gn_ref[...])
        y_ref[pl.ds(c0, CHUNK), :] = o.astype(y_ref.dtype)
        return carry

    lax.fori_loop(0, nchunk, finish, 0)


def _retention(proj, log_g, gn, rope, init_state, row_off, nb, seq, positional, write_state):
    rb = row_off // seq
    has_init = init_state is not None
    in_specs = [
        pl.BlockSpec(memory_space=pltpu.SMEM),
        pl.BlockSpec((seq, DK), lambda b, h: (rb + b, h)),
        pl.BlockSpec((seq, DK), lambda b, h: (rb + b, HEADS + h)),
        pl.BlockSpec((seq, DV), lambda b, h: (rb + b, HEADS + h)),
        pl.BlockSpec((seq, DV), lambda b, h: (rb + b, 2 * HEADS + h)),
        pl.BlockSpec((1, DV), lambda b, h: (0, h)),
    ]
    args = [log_g, proj, proj, proj, proj, gn]
    if positional:
        in_specs += [pl.BlockSpec((seq, DK), lambda b, h: (0, 0))] * 2
        args += list(rope)
    if has_init:
        in_specs.append(pl.BlockSpec((None, 2, None, DK, DV), lambda b, h: (b, 0, h, 0, 0)))
        args.append(init_state)
    out_specs = [pl.BlockSpec((seq, DV), lambda b, h: (b, h))]
    out_shape = [jax.ShapeDtypeStruct((nb * seq, VD), BF16)]
    if write_state:
        out_specs.append(pl.BlockSpec((None, 2, None, DK, DV), lambda b, h: (b, 0, h, 0, 0)))
        out_shape.append(jax.ShapeDtypeStruct((nb, 2, HEADS, DK, DV), F32))
    res = pl.pallas_call(
        functools.partial(_ret_kernel, seq=seq, positional=positional, has_init=has_init,
                          write_state=write_state),
        grid=(nb, HEADS),
        in_specs=in_specs, out_specs=out_specs, out_shape=out_shape,
        scratch_shapes=[pltpu.VMEM((seq, DK), F32), pltpu.VMEM((seq, DK), F32),
                        pltpu.VMEM((seq, DV), F32), pltpu.VMEM((seq, DV), F32),
                        pltpu.VMEM((DK, DV), F32), pltpu.VMEM((DK, DV), F32)],
        compiler_params=_cparams(("parallel", "parallel")),
        name="retention_s%d" % seq,
    )(*args)
    return res


def _rope_tables(seq):
    half = DK // 4
    s = jnp.arange(seq, dtype=jnp.int32)
    freqs = ROPE_BASE ** (-jnp.arange(half, dtype=F32) / half)
    tabs_c, tabs_s = [], []
    for pos in ((s // GRID_W).astype(F32), (s % GRID_W).astype(F32)):
        ang = pos[:, None] * freqs[None, :]
        c, sn = jnp.cos(ang), jnp.sin(ang)
        tabs_c += [c, c]
        tabs_s += [-sn, sn]
    return jnp.concatenate(tabs_c, axis=1), jnp.concatenate(tabs_s, axis=1)


def _dft_mats(n):
    j = jnp.arange(n, dtype=jnp.int32)
    ang = ((j[:, None] * j[None, :]) % n).astype(F32) * (2.0 * np.pi / n)
    return jnp.cos(ang), jnp.sin(ang)


def _dft1_kernel(h_ref, w_ref, a_ref, b_ref):
    w = w_ref[...]
    for g in range(FNET_GROUPS):
        sl = slice(g * FNET_C, (g + 1) * FNET_C)
        r = _dot(h_ref[:, sl], w)
        a_ref[:, sl] = r[:, :FNET_C].astype(a_ref.dtype)
        b_ref[:, sl] = r[:, FNET_C:].astype(b_ref.dtype)


def _dft1(h, wcs):
    tm = 512
    return pl.pallas_call(
        _dft1_kernel,
        grid=(N_TOK // tm,),
        in_specs=[pl.BlockSpec((tm, D), lambda i: (i, 0)),
                  pl.BlockSpec((FNET_C, 2 * FNET_C), lambda i: (0, 0))],
        out_specs=[pl.BlockSpec((tm, D), lambda i: (i, 0))] * 2,
        out_shape=[jax.ShapeDtypeStruct((N_TOK, D), BF16)] * 2,
        compiler_params=_cparams(("parallel",)),
        name="dft_channels",
    )(h, wcs)


def _dft2_kernel(c_ref, s_ref, a_ref, b_ref, f_ref, *, scale):
    f = _dot(c_ref[...], a_ref[...]) - _dot(s_ref[...], b_ref[...])
    f_ref[...] = (f * scale).astype(f_ref.dtype)


def _dft2(a, b, cmat, smat, row_off, nb, seq):
    tr = 256
    nr = seq // tr
    sb = row_off // seq
    return pl.pallas_call(
        functools.partial(_dft2_kernel, scale=float(1.0 / np.sqrt(seq * FNET_C))),
        grid=(nb, nr),
        in_specs=[pl.BlockSpec((tr, seq), lambda bi, r: (r, 0)),
                  pl.BlockSpec((tr, seq), lambda bi, r: (r, 0)),
                  pl.BlockSpec((seq, D), lambda bi, r: (sb + bi, 0)),
                  pl.BlockSpec((seq, D), lambda bi, r: (sb + bi, 0))],
        out_specs=pl.BlockSpec((tr, D), lambda bi, r: (bi * nr + r, 0)),
        out_shape=jax.ShapeDtypeStruct((nb * seq, D), BF16),
        compiler_params=_cparams(("parallel", "parallel")),
        name="dft_positions_s%d" % seq,
    )(cmat, smat, a, b)


CONV_HALO = 16
CONV_PAD = CONV_WIDTH // 2


def _dwconv_kernel(prev_ref, cur_ref, next_ref, w_ref, b_ref, lg_ref, lb_ref, o_ref, win, cv, *, ts, nr):
    r = pl.program_id(1)
    zero = jnp.zeros((CONV_HALO, D), F32)
    win[0:CONV_HALO, :] = jnp.where(r > 0, prev_ref[...], zero)
    win[CONV_HALO:CONV_HALO + ts, :] = cur_ref[...]
    win[CONV_HALO + ts:, :] = jnp.where(r < nr - 1, next_ref[...], zero)
    lanes = 256
    rows = 128

    def col_body(cj, carry):
        c0 = pl.multiple_of(cj * lanes, lanes)
        for r0 in range(0, ts, rows):
            acc = jnp.broadcast_to(b_ref[:, pl.ds(c0, lanes)], (rows, lanes))
            for t in range(CONV_WIDTH):
                start = CONV_HALO - CONV_PAD + t + r0
                acc = acc + w_ref[t:t + 1, pl.ds(c0, lanes)] * win[start:start + rows, pl.ds(c0, lanes)]
            cv[r0:r0 + rows, pl.ds(c0, lanes)] = acc
        return carry

    lax.fori_loop(0, D // lanes, col_body, 0)
    u = cv[...]
    mu = jnp.mean(u, axis=-1, keepdims=True)
    uc = u - mu
    var = jnp.mean(uc * uc, axis=-1, keepdims=True)
    y = uc * lax.rsqrt(var + EPS) * lg_ref[...] + lb_ref[...]
    o_ref[...] = (y * _sigmoid(y)).astype(o_ref.dtype)


def _dwconv(u, wdw, bdw, ln_g, ln_b, row_off, nb, seq):
    ts = 256
    nr = seq // ts
    hb = ts // CONV_HALO
    tb = row_off // ts
    n_hblocks = N_TOK // CONV_HALO

    def cur(bi, r):
        return (tb + bi * nr + r, 0)

    def prev(bi, r):
        return (jnp.maximum((tb + bi * nr + r) * hb - 1, 0), 0)

    def nxt(bi, r):
        return (jnp.minimum((tb + bi * nr + r + 1) * hb, n_hblocks - 1), 0)

    const = lambda bi, r: (0, 0)
    return pl.pallas_call(
        functools.partial(_dwconv_kernel, ts=ts, nr=nr),
        grid=(nb, nr),
        in_specs=[pl.BlockSpec((CONV_HALO, D), prev), pl.BlockSpec((ts, D), cur),
                  pl.BlockSpec((CONV_HALO, D), nxt),
                  pl.BlockSpec((CONV_WIDTH + 1, D), const), pl.BlockSpec((1, D), const),
                  pl.BlockSpec((1, D), const), pl.BlockSpec((1, D), const)],
        out_specs=pl.BlockSpec((ts, D), lambda bi, r: (bi * nr + r, 0)),
        out_shape=jax.ShapeDtypeStruct((nb * seq, D), BF16),
        scratch_shapes=[pltpu.VMEM((ts + 2 * CONV_HALO, D), F32), pltpu.VMEM((ts, D), F32)],
        compiler_params=_cparams(("parallel", "parallel")),
        name="dwconv_s%d" % seq,
    )(u, u, u, wdw, bdw, ln_g, ln_b)


POOL_HALO = 8


def _pool_kernel(prev_ref, cur_ref, next_ref, w_ref, ps_ref, *refs, ts, seq, nr):
    win, yv = refs[-2], refs[-1]
    r = pl.program_id(1)
    win[0:POOL_HALO, :] = prev_ref[...]
    win[POOL_HALO:POOL_HALO + ts, :] = cur_ref[...]
    win[POOL_HALO + ts:, :] = next_ref[...]
    pos = r * ts + lax.broadcasted_iota(jnp.int32, (ts, 1), 0)
    for gi, wd in enumerate(POOL_WINDOWS):
        sl = slice(gi * POOL_GROUP, (gi + 1) * POOL_GROUP)
        half = wd // 2
        acc = jnp.zeros((ts, POOL_GROUP), F32)
        cnt = jnp.zeros((ts, 1), F32)
        for dlt in range(-half, half):
            ok = jnp.logical_and(pos + dlt >= 0, pos + dlt <= seq - 1)
            acc = acc + jnp.where(ok, win[POOL_HALO + dlt:POOL_HALO + dlt + ts, sl], 0.0)
            cnt = cnt + jnp.where(ok, 1.0, 0.0)
        pooled = acc / cnt - cur_ref[:, sl]
        yv[:, sl] = _dot(pooled.astype(BF16), w_ref[gi].astype(BF16))
    out = yv[...] * ps_ref[...]
    _epilogue_from_refs(out, refs[:-2], True, True)


def _pool(h, x, w_pool, pscale, gpost, mod_l, gpre, router, row_off, nb, seq):
    ts = 256
    nr = seq // ts
    hb = ts // POOL_HALO
    tb = row_off // ts
    n_hblocks = N_TOK // POOL_HALO

    def tile(bi, r):
        return tb + bi * nr + r

    def cur(bi, r):
        return (tile(bi, r), 0)

    def prev(bi, r):
        return (jnp.maximum(tile(bi, r) * hb - 1, 0), 0)

    def nxt(bi, r):
        return (jnp.minimum((tile(bi, r) + 1) * hb, n_hblocks - 1), 0)

    ep_in, ep_out, ep_shape = _epilogue_specs(ts, tile, 2, 3, True, True, BF16)
    ep_out = [pl.BlockSpec(s.block_shape, lambda bi, r: (bi * nr + r, 0)) for s in ep_out]
    ep_shape = [jax.ShapeDtypeStruct((nb * seq,) + s.shape[1:], s.dtype) for s in ep_shape]
    const = lambda bi, r: (0, 0)
    return pl.pallas_call(
        functools.partial(_pool_kernel, ts=ts, seq=seq, nr=nr),
        grid=(nb, nr),
        in_specs=[pl.BlockSpec((POOL_HALO, D), prev), pl.BlockSpec((ts, D), cur),
                  pl.BlockSpec((POOL_HALO, D), nxt),
                  pl.BlockSpec((len(POOL_WINDOWS), POOL_GROUP, POOL_GROUP), lambda bi, r: (0, 0, 0)),
                  pl.BlockSpec((1, D), const)] + ep_in,
        out_specs=ep_out, out_shape=ep_shape,
        scratch_shapes=[pltpu.VMEM((ts + 2 * POOL_HALO, D), F32), pltpu.VMEM((ts, D), F32)],
        compiler_params=_cparams(("parallel", "parallel")),
        name="pool_s%d" % seq,
    )(h, h, h, w_pool, pscale, *_epilogue_args(x, gpost, mod_l, gpre, mod_l, router))


FFN_ROWS = 512


def _ffn_up_kernel(xe_ref, wg_ref, wu_ref, hid_ref, wg_s, wu_s):
    wg_s[...] = wg_ref[...].astype(BF16)
    wu_s[...] = wu_ref[...].astype(BF16)

    def body(i, carry):
        r0 = pl.multiple_of(i * FFN_ROWS, FFN_ROWS)
        xe = xe_ref[pl.ds(r0, FFN_ROWS), :]
        a = _dot(xe, wg_s[...])
        b = _dot(xe, wu_s[...])
        hid_ref[pl.ds(r0, FFN_ROWS), :] = ((a * _sigmoid(a)) * b).astype(hid_ref.dtype)
        return carry

    lax.fori_loop(0, CAP_T // FFN_ROWS, body, 0)


def _ffn_down_kernel(hid_ref, wd_ref, ye_ref, wd_s):
    wd_s[...] = wd_ref[...].astype(BF16)

    def body(i, carry):
        r0 = pl.multiple_of(i * FFN_ROWS, FFN_ROWS)
        y = _dot(hid_ref[pl.ds(r0, FFN_ROWS), :], wd_s[...])
        ye_ref[pl.ds(r0, FFN_ROWS), :] = y.astype(ye_ref.dtype)
        return carry

    lax.fori_loop(0, CAP_T // FFN_ROWS, body, 0)


def _expert_ffn(xe, layer, wg, wu, wd):
    tf, td = 512, 512
    hid = pl.pallas_call(
        _ffn_up_kernel,
        grid=(N_EXPERTS, D_FF // tf),
        in_specs=[pl.BlockSpec((None, CAP_T, D), lambda e, f: (e, 0, 0)),
                  pl.BlockSpec((None, None, D, tf), lambda e, f: (layer, e, 0, f)),
                  pl.BlockSpec((None, None, D, tf), lambda e, f: (layer, e, 0, f))],
        out_specs=pl.BlockSpec((None, CAP_T, tf), lambda e, f: (e, 0, f)),
        out_shape=jax.ShapeDtypeStruct((N_EXPERTS, CAP_T, D_FF), BF16),
        scratch_shapes=[pltpu.VMEM((D, tf), BF16), pltpu.VMEM((D, tf), BF16)],
        compiler_params=_cparams(("parallel", "parallel")),
        name="ffn_up",
    )(xe, wg, wu)
    return pl.pallas_call(
        _ffn_down_kernel,
        grid=(N_EXPERTS, D // td),
        in_specs=[pl.BlockSpec((None, CAP_T, D_FF), lambda e, j: (e, 0, 0)),
                  pl.BlockSpec((None, None, D_FF, td), lambda e, j: (layer, e, 0, j))],
        out_specs=pl.BlockSpec((None, CAP_T, td), lambda e, j: (e, 0, j)),
        out_shape=jax.ShapeDtypeStruct((N_EXPERTS, CAP_T, D), BF16),
        scratch_shapes=[pltpu.VMEM((D_FF, td), BF16)],
        compiler_params=_cparams(("parallel", "parallel")),
        name="ffn_down",
    )(hid, wd)


RB = 256
N_RB = N_TOK // RB


def _route_kernel(aff_ref, slot_ref, base_ref, cnt_ref, key_s, *, n, cap, slot_off):
    nb = n // RB
    key_s[...] = pltpu.bitcast(aff_ref[...], jnp.int32)

    def bit_body(i, t):
        cand = t | jnp.left_shift(jnp.int32(1), 30 - i)
        cnt = jnp.sum((key_s[...] >= cand).astype(jnp.int32), axis=0, keepdims=True)
        return jnp.where(cnt >= cap, cand, t)

    t = lax.fori_loop(0, 31, bit_body, jnp.zeros((1, ROUTER_LANES), jnp.int32))
    c_gt = jnp.sum((key_s[...] > t).astype(jnp.int32), axis=0, keepdims=True)
    need = (cap - c_gt).astype(F32)
    lane_ok = lax.broadcasted_iota(jnp.int32, (1, ROUTER_LANES), 1) < N_EXPERTS
    tri = (lax.broadcasted_iota(jnp.int32, (RB, RB), 0) >
           lax.broadcasted_iota(jnp.int32, (RB, RB), 1)).astype(BF16)

    def blk(b, carry):
        eq_carry, sel_carry = carry
        r0 = pl.multiple_of(b * RB, RB)
        k = key_s[pl.ds(r0, RB), :]
        eq = k == t
        eq_rank = _dot(tri, eq.astype(BF16)) + eq_carry
        sel = jnp.logical_or(k > t, jnp.logical_and(eq, eq_rank < need))
        self_f = sel.astype(F32)
        pos = _dot(tri, sel.astype(BF16)) + sel_carry
        slot = jnp.where(jnp.logical_and(sel, lane_ok), pos.astype(jnp.int32) + slot_off, -1)
        slot_ref[pl.ds(r0, RB), :] = slot
        n_sel = jnp.sum(self_f, axis=0, keepdims=True)
        base_ref[pl.ds(b, 1), :] = sel_carry.astype(jnp.int32) + slot_off
        cnt_ref[pl.ds(b, 1), :] = n_sel.astype(jnp.int32)
        return eq_carry + jnp.sum(eq.astype(F32), axis=0, keepdims=True), sel_carry + n_sel

    zero = jnp.zeros((1, ROUTER_LANES), F32)
    lax.fori_loop(0, nb, blk, (zero, zero))


def _route(aff_g, cap, slot_off):
    n = aff_g.shape[0]
    return pl.pallas_call(
        functools.partial(_route_kernel, n=n, cap=cap, slot_off=slot_off),
        out_shape=[jax.ShapeDtypeStruct((n, ROUTER_LANES), jnp.int32),
                   jax.ShapeDtypeStruct((n // RB, ROUTER_LANES), jnp.int32),
                   jax.ShapeDtypeStruct((n // RB, ROUTER_LANES), jnp.int32)],
        scratch_shapes=[pltpu.VMEM((n, ROUTER_LANES), jnp.int32)],
        compiler_params=pltpu.CompilerParams(vmem_limit_bytes=VMEM_LIMIT),
        name="route_n%d" % n,
    )(aff_g)


GATHER_G = 2
GATHER_WIN_LOG2 = 6
GATHER_WIN = 1 << GATHER_WIN_LOG2
GATHER_STACK = 2


def _gather_kernel(base_ref, cnt_ref, slot_ref, h_ref, xe_ref):
    eg, j = pl.program_id(0), pl.program_id(1)

    @pl.when(j == 0)
    def _():
        xe_ref[...] = jnp.zeros_like(xe_ref)

    sio = lax.broadcasted_iota(jnp.int32, (GATHER_WIN, RB), 0)

    def onehot(tok_slot, r0):
        return (tok_slot - r0 == sio).astype(BF16)

    runs = []
    for u in range(GATHER_G):
        e = eg * GATHER_G + u
        base = base_ref[e * N_RB + j]
        cnt = cnt_ref[e * N_RB + j]
        first = base & (-GATHER_WIN)
        n_win = jnp.where(cnt > 0,
                          lax.shift_right_logical(base + cnt - first + GATHER_WIN - 1, GATHER_WIN_LOG2), 0)
        runs.append((u, first, n_win, slot_ref[pl.ds(e, 1), :]))

    stacked = jnp.concatenate([onehot(ts, first + w * GATHER_WIN)
                               for (_, first, _, ts) in runs for w in range(GATHER_STACK)], axis=0)
    rows = _dot(stacked, h_ref[...]).astype(xe_ref.dtype)
    for (u, first, n_win, ts) in runs:
        for w in range(GATHER_STACK):
            r0 = pl.multiple_of(jnp.minimum(first + w * GATHER_WIN, CAP_T - GATHER_WIN), GATHER_WIN)
            i0 = (u * GATHER_STACK + w) * GATHER_WIN
            xe_ref[u, pl.ds(r0, GATHER_WIN), :] += rows[i0:i0 + GATHER_WIN]

        def window(w, carry):
            r0 = pl.multiple_of(first + w * GATHER_WIN, GATHER_WIN)
            xe_ref[u, pl.ds(r0, GATHER_WIN), :] += _dot(onehot(ts, r0), h_ref[...]).astype(xe_ref.dtype)
            return carry

        lax.fori_loop(GATHER_STACK, n_win, window, 0)


def _gather(h, slot_t, base_tbl, cnt_tbl):
    return pl.pallas_call(
        _gather_kernel,
        grid_spec=pltpu.PrefetchScalarGridSpec(
            num_scalar_prefetch=2,
            grid=(N_EXPERTS // GATHER_G, N_RB),
            in_specs=[pl.BlockSpec((N_EXPERTS, RB), lambda eg, j, b, c: (0, j)),
                      pl.BlockSpec((RB, D), lambda eg, j, b, c: (j, 0))],
            out_specs=pl.BlockSpec((GATHER_G, CAP_T, D), lambda eg, j, b, c: (eg, 0, 0)),
        ),
        out_shape=jax.ShapeDtypeStruct((N_EXPERTS, CAP_T, D), BF16),
        compiler_params=_cparams(("parallel", "arbitrary")),
        name="moe_gather",
    )(base_tbl, cnt_tbl, slot_t, h)


BF16_EXACT_LOG2 = 8
CHUNK_LOG2 = 5
CHUNK_SLOTS = 1 << CHUNK_LOG2
GROUP_LOG2 = 3
CHUNKS_PER_GROUP = 1 << GROUP_LOG2
assert CHUNKS_PER_GROUP * CHUNK_SLOTS == RB
MAX_CHUNKS = N_EXPERTS * (RB // CHUNK_SLOTS + 1)
EXPERT_CHUNKS = CAP_T // CHUNK_SLOTS


def _combine_kernel(nch_ref, che_ref, chc_ref, slot_ref, gate_ref, ye_hbm, *refs, has_pre):
    stage, sems, y_acc = refs[-3], refs[-2], refs[-1]
    j = pl.program_id(0)
    n = nch_ref[j]
    n_groups = lax.shift_right_logical(n + CHUNKS_PER_GROUP - 1, GROUP_LOG2)

    @pl.when(j == 0)
    def _():
        stage[...] = jnp.zeros_like(stage)

    def chunk_copy(g, q, buf):
        k = j * MAX_CHUNKS + g * CHUNKS_PER_GROUP + q
        src = ye_hbm.at[che_ref[k], pl.ds(pl.multiple_of(chc_ref[k] * CHUNK_SLOTS, CHUNK_SLOTS), CHUNK_SLOTS), :]
        return pltpu.make_async_copy(src, stage.at[buf, pl.ds(q * CHUNK_SLOTS, CHUNK_SLOTS), :], sems.at[buf])

    def for_group(g, buf, fn):
        for q in range(CHUNKS_PER_GROUP):
            @pl.when(g * CHUNKS_PER_GROUP + q < n)
            def _():
                fn(chunk_copy(g, q, buf))

    slot1 = slot_ref[...] + 1
    s_hi = lax.shift_right_logical(slot1, BF16_EXACT_LOG2).astype(F32).astype(BF16)
    s_lo = (slot1 & ((1 << BF16_EXACT_LOG2) - 1)).astype(F32).astype(BF16)
    gate = gate_ref[...]
    g_hi = gate.astype(BF16)
    g_lo = (gate - g_hi.astype(F32)).astype(BF16)
    lane = lax.broadcasted_iota(jnp.int32, (1, RB), 1)
    pos_in_group = lax.shift_right_logical(lane, CHUNK_LOG2)
    expert_iota = lax.broadcasted_iota(jnp.int32, (ROUTER_LANES, RB), 0)
    y_acc[...] = jnp.zeros_like(y_acc)

    @pl.when(n_groups > 0)
    def _():
        for_group(0, 0, lambda cp: cp.start())

    def group_body(g, carry):
        buf = g & 1

        @pl.when(g + 1 < n_groups)
        def _():
            for_group(g + 1, 1 - buf, lambda cp: cp.start())

        for_group(g, buf, lambda cp: cp.wait())
        want = jnp.full((1, RB), -1, jnp.int32)
        owner = jnp.zeros((1, RB), jnp.int32)
        for q in range(CHUNKS_PER_GROUP):
            kq = g * CHUNKS_PER_GROUP + q
            k = j * MAX_CHUNKS + kq
            first = chc_ref[k] * CHUNK_SLOTS + 1 - q * CHUNK_SLOTS
            want = jnp.where(pos_in_group == q, jnp.where(kq < n, lane + first, -1), want)
            owner = jnp.where(pos_in_group == q, che_ref[k], owner)
        pick = (expert_iota == owner).astype(BF16)
        tok_slot = float(1 << BF16_EXACT_LOG2) * _dot(s_hi, pick) + _dot(s_lo, pick)
        tok_gate = _dot(g_hi, pick) + _dot(g_lo, pick)
        w = jnp.where(tok_slot == want.astype(F32), tok_gate, 0.0)
        w_hi = w.astype(BF16)
        w_lo = (w - w_hi.astype(F32)).astype(BF16)
        rows = stage[buf]
        y_acc[...] += _dot(w_hi, rows) + _dot(w_lo, rows)
        return carry

    lax.fori_loop(0, n_groups, group_body, 0)
    _epilogue_from_refs(y_acc[...], refs[:-3], has_pre, False)


def _chunk_tables(base_tbl, cnt_tbl):
    base = base_tbl.reshape(N_EXPERTS, N_RB).T
    cnt = cnt_tbl.reshape(N_EXPERTS, N_RB).T
    first = lax.shift_right_logical(base, CHUNK_LOG2)
    last = lax.shift_right_logical(base + cnt - 1, CHUNK_LOG2)
    per_expert = jnp.where(cnt > 0, last - first + 1, 0)
    ends = jnp.cumsum(per_expert, axis=1)
    starts = ends - per_expert
    k = jnp.arange(MAX_CHUNKS, dtype=jnp.int32)[None, :]
    owner = jnp.sum((k[:, :, None] >= ends[:, None, :]).astype(jnp.int32), axis=2)
    owner = jnp.minimum(owner, N_EXPERTS - 1)
    chunk = jnp.take_along_axis(first, owner, axis=1) + k - jnp.take_along_axis(starts, owner, axis=1)
    chunk = jnp.clip(chunk, 0, EXPERT_CHUNKS - 1)
    return ends[:, -1], owner.reshape(-1), chunk.reshape(-1)


def _combine_post(ye, slot, aff, base_tbl, cnt_tbl, x, gpost, mod_l, gpre, mod_pre, h_dtype):
    nch, che, chc = _chunk_tables(base_tbl, cnt_tbl)
    has_pre = gpre is not None
    tok = lambda j, *_: (j, 0)
    ep_in, ep_out, ep_shape = _epilogue_specs(RB, lambda j, *_: j, 5, 0, has_pre, False, h_dtype)
    res = pl.pallas_call(
        functools.partial(_combine_kernel, has_pre=has_pre),
        grid_spec=pltpu.PrefetchScalarGridSpec(
            num_scalar_prefetch=3,
            grid=(N_RB,),
            in_specs=[pl.BlockSpec((RB, ROUTER_LANES), tok),
                      pl.BlockSpec((RB, ROUTER_LANES), tok),
                      pl.BlockSpec(memory_space=pl.ANY)] + ep_in,
            out_specs=ep_out,
            scratch_shapes=[pltpu.VMEM((2, RB, D), BF16), pltpu.SemaphoreType.DMA((2,)),
                            pltpu.VMEM((RB, D), F32)],
        ),
        out_shape=ep_shape,
        compiler_params=_cparams(("arbitrary",)),
        name="moe_combine",
    )(nch, che, chc, slot, aff, ye, *_epilogue_args(x, gpost, mod_l, gpre, mod_pre, None))
    return res if has_pre else (res[0], None)


def _moe(h, aff, layer, wg, wu, wd, x, gpost, mod_l, gpre, mod_pre, h_dtype):
    slot_p, base_p, cnt_p = _route(aff[:N_PROMPT], CAP_P, 0)
    slot_s, base_s, cnt_s = _route(aff[N_PROMPT:], CAP_S, CAP_P)
    slot = jnp.concatenate([slot_p, slot_s], axis=0)

    def table(tp, ts):
        return jnp.concatenate([tp, ts], axis=0)[:, :N_EXPERTS].T.reshape(-1)

    base_tbl, cnt_tbl = table(base_p, base_s), table(cnt_p, cnt_s)
    xe = _gather(h, slot[:, :N_EXPERTS].T, base_tbl, cnt_tbl)
    ye = _expert_ffn(xe, layer, wg, wu, wd)
    return _combine_post(ye, slot, aff, base_tbl, cnt_tbl, x, gpost, mod_l, gpre, mod_pre, h_dtype)


def kernel(x_prompt, x_sample, state_ret, c, c_ctx, w_mod, b_mod, g_norm, ret_w_in, ret_w_out, ret_decay,
           ret_gn, fnet_w, conv_w1, conv_wdw, conv_bdw, conv_ln_g, conv_ln_b, conv_w2, pool_w, pool_scale,
           moe_router, moe_w_gate, moe_w_up, moe_w_down):
    x = jnp.concatenate([x_prompt.reshape(N_PROMPT, D), x_sample.reshape(N_SAMPLE, D)], axis=0)
    cond_all = jnp.concatenate([c_ctx[None, :], c, jnp.zeros((MOD_ROWS - 1 - DEC_BATCH, D), F32)], axis=0)
    mod = _mod_table(cond_all, w_mod, b_mod)
    groups = ((0, BATCH, SEQ), (N_PROMPT, DEC_BATCH, DEC_SEQ))

    def gn(i, k):
        return g_norm[i, k][None, :]

    def router(i):
        wr = jnp.pad(moe_router[i], ((0, 0), (0, ROUTER_LANES - N_EXPERTS)))
        wrh = wr.astype(BF16)
        wrl = (wr - wrh.astype(F32)).astype(BF16)
        return wrh, wrl

    new_state = None
    h = _prenorm(x, gn(0, 0), mod[0], 0, BF16)
    for i in range(DEPTH):
        mod_l = mod[i]
        kind = i % 4
        if kind == 0:
            proj = _mm_ws(h, ret_w_in[0], IN_W, False, 512, 1024, F32, "ret_in_proj")
            log_g = jax.nn.log_sigmoid(ret_decay[0].astype(F32))
            ys = []
            for (row_off, nb, seq) in groups:
                positional = row_off != 0
                res = _retention(proj, log_g, ret_gn[0][None, :], _rope_tables(seq) if positional else None,
                                 state_ret[:, 0] if positional else None, row_off, nb, seq, positional,
                                 not positional)
                ys.append(res[0])
                if not positional:
                    new_state = res[1]
            x, h, aff = _mm_post(ys[0], ys[1], ret_w_out[0].astype(BF16), x, gn(i, 1), mod_l, 2, gn(i, 2), 3,
                                 router(i), BF16, "ret_out_proj")
        elif kind == 1:
            cc, sc = _dft_mats(FNET_C)
            wcs = jnp.concatenate([cc, sc], axis=1).astype(BF16)
            fa, fb = _dft1(h, wcs)
            fs = []
            for (row_off, nb, seq) in groups:
                cm, sm = _dft_mats(seq)
                fs.append(_dft2(fa, fb, cm.astype(BF16), sm.astype(BF16), row_off, nb, seq))
            x, h, aff = _mm_post(fs[0], fs[1], fnet_w[0].astype(BF16), x, gn(i, 1), mod_l, 2, gn(i, 2), 3,
                                 router(i), BF16, "fnet_proj")
        elif kind == 2:
            u = _mm_ws(h, conv_w1[0], D, True, 512, 512, F32, "conv_glu")
            wdw = jnp.pad(conv_wdw[0], ((0, 1), (0, 0)))
            vs = [_dwconv(u, wdw, conv_bdw[0][None, :], conv_ln_g[0][None, :], conv_ln_b[0][None, :],
                          row_off, nb, seq) for (row_off, nb, seq) in groups]
            x, h, aff = _mm_post(vs[0], vs[1], conv_w2[0].astype(BF16), x, gn(i, 1), mod_l, 2, gn(i, 2), 3,
                                 router(i), BF16, "conv_out_proj")
        else:
            outs = [_pool(h, x, pool_w[0], pool_scale[0][None, :], gn(i, 1), mod_l, gn(i, 2), router(i),
                          row_off, nb, seq) for (row_off, nb, seq) in groups]
            x = jnp.concatenate([o[0] for o in outs], axis=0)
            h = jnp.concatenate([o[1] for o in outs], axis=0)
            aff = jnp.concatenate([o[2] for o in outs], axis=0)
        if i + 1 < DEPTH:
            nxt_dtype = F32 if (i + 1) % 4 == 3 else BF16
            x, h = _moe(h, aff, i, moe_w_gate, moe_w_up, moe_w_down, x, gn(i, 3), mod_l, gn(i + 1, 0),
                        mod[i + 1], nxt_dtype)
        else:
            x, h = _moe(h, aff, i, moe_w_gate, moe_w_up, moe_w_down, x, gn(i, 3), mod_l, None, None, BF16)
    y_prompt = x[:N_PROMPT].reshape(BATCH, SEQ, D)
    y_sample = x[N_PROMPT:].reshape(DEC_BATCH, DEC_SEQ, D)
    return y_prompt, y_sample, new_state[:, None]
```

```python
import functools

import jax
import jax.numpy as jnp
import numpy as np
from jax import lax
from jax.experimental import pallas as pl
from jax.experimental.pallas import tpu as pltpu

F32 = jnp.float32
BF16 = jnp.bfloat16

D = 2048
BATCH, SEQ = 16, 256
DEC_BATCH, DEC_SEQ = 8, 1024
N_PROMPT = BATCH * SEQ
N_SAMPLE = DEC_BATCH * DEC_SEQ
N_TOK = N_PROMPT + N_SAMPLE
DEPTH = 4
GRID_W = 64
HEADS = 8
DK = D // HEADS
DV = 2 * DK
CHUNK = 128
ROPE_BASE = 10000.0
FNET_GROUPS = 4
FNET_C = D // FNET_GROUPS
CONV_WIDTH = 31
POOL_WINDOWS = (2, 4, 8, 16)
POOL_GROUP = D // len(POOL_WINDOWS)
N_EXPERTS = 16
CAP_P = 2 * N_PROMPT // N_EXPERTS
CAP_S = 2 * N_SAMPLE // N_EXPERTS
CAP_T = CAP_P + CAP_S
D_FF = 2 * D
N_MOD = 6
EPS = 1e-6
IN_W = 2 * HEADS * DK + 2 * HEADS * DV
VD = HEADS * DV
MOD_ROWS = 16
ROUTER_LANES = 128

VMEM_LIMIT = 56 * 1024 * 1024


def _cparams(sem):
    return pltpu.CompilerParams(dimension_semantics=sem, vmem_limit_bytes=VMEM_LIMIT)


def _sigmoid(x):
    return 1.0 / (1.0 + jnp.exp(-x))


def _dot(a, b):
    return jnp.dot(a, b, preferred_element_type=F32)


def _mod_row(row0):
    return jnp.where(row0 < N_PROMPT, 0, 1 + (row0 - N_PROMPT) // DEC_SEQ)


def _rms(v, g):
    ms = jnp.mean(v * v, axis=-1, keepdims=True)
    return v * lax.rsqrt(ms + EPS) * g


def _mod_kernel(c_ref, w_ref, b_ref, o_ref):
    c = c_ref[...]
    s = (c * _sigmoid(c)).astype(BF16)
    o_ref[...] = _dot(s, w_ref[...].astype(BF16)) + b_ref[...]


def _mod_table(cond_all, w_mod, b_mod):
    tn = 1024
    out = pl.pallas_call(
        _mod_kernel,
        grid=(DEPTH, N_MOD * D // tn),
        in_specs=[
            pl.BlockSpec((MOD_ROWS, D), lambda l, j: (0, 0)),
            pl.BlockSpec((None, D, tn), lambda l, j: (l, 0, j)),
            pl.BlockSpec((None, 1, tn), lambda l, j: (l, 0, j)),
        ],
        out_specs=pl.BlockSpec((None, MOD_ROWS, tn), lambda l, j: (l, 0, j)),
        out_shape=jax.ShapeDtypeStruct((DEPTH, MOD_ROWS, N_MOD * D), F32),
        compiler_params=_cparams(("parallel", "parallel")),
        name="mod_table",
    )(cond_all, w_mod, b_mod.reshape(DEPTH, 1, N_MOD * D))
    return out.reshape(DEPTH, MOD_ROWS * N_MOD, 1, D)


def _post_pre(out, x, gpost, gt, pre, router):
    xn = x + gt * _rms(out, gpost)
    if pre is None:
        return xn, None, None
    gpre, sc, sh = pre
    h = _rms(xn, gpre) * (1.0 + sc) + sh
    if router is None:
        return xn, h, None
    wrh, wrl = router
    hh = h.astype(BF16)
    hl = (h - hh.astype(F32)).astype(BF16)
    lg = _dot(hh, wrh) + _dot(hh, wrl) + _dot(hl, wrh)
    lane = lax.broadcasted_iota(jnp.int32, lg.shape, 1)
    lg = jnp.where(lane < N_EXPERTS, lg, -jnp.inf)
    e = jnp.exp(lg - jnp.max(lg, axis=-1, keepdims=True))
    aff = e / jnp.sum(e, axis=-1, keepdims=True)
    return xn, h, aff


def _epilogue_from_refs(out, refs, has_pre, has_router):
    it = iter(refs)
    x_ref, gpost_ref, gt_ref = next(it), next(it), next(it)
    pre = router = None
    if has_pre:
        pre = (next(it)[...], next(it)[...], next(it)[...])
    if has_router:
        router = (next(it)[...], next(it)[...])
    xo_ref = next(it)
    xn, h, aff = _post_pre(out, x_ref[...], gpost_ref[...], gt_ref[...], pre, router)
    xo_ref[...] = xn
    if has_pre:
        h_ref = next(it)
        h_ref[...] = h.astype(h_ref.dtype)
    if has_router:
        aff_ref = next(it)
        aff_ref[...] = aff


def _epilogue_specs(tm, row_imap, which_gt, which_pre, has_pre, has_router, h_dtype, n_rows=N_TOK,
                    mod_row0=None):
    if mod_row0 is None:
        mod_row0 = lambda *idx: row_imap(*idx) * tm

    def rows(*idx):
        return (row_imap(*idx), 0)

    def const(*idx):
        return (0, 0)

    def modmap(which):
        return lambda *idx: (_mod_row(mod_row0(*idx)) * N_MOD + which, 0, 0)

    in_specs = [pl.BlockSpec((tm, D), rows), pl.BlockSpec((1, D), const),
                pl.BlockSpec((None, 1, D), modmap(which_gt))]
    if has_pre:
        in_specs += [pl.BlockSpec((1, D), const),
                     pl.BlockSpec((None, 1, D), modmap(which_pre + 1)),
                     pl.BlockSpec((None, 1, D), modmap(which_pre))]
    if has_router:
        in_specs += [pl.BlockSpec((D, ROUTER_LANES), const), pl.BlockSpec((D, ROUTER_LANES), const)]
    out_specs = [pl.BlockSpec((tm, D), rows)]
    out_shape = [jax.ShapeDtypeStruct((n_rows, D), F32)]
    if has_pre:
        out_specs.append(pl.BlockSpec((tm, D), rows))
        out_shape.append(jax.ShapeDtypeStruct((n_rows, D), h_dtype))
    if has_router:
        out_specs.append(pl.BlockSpec((tm, ROUTER_LANES), rows))
        out_shape.append(jax.ShapeDtypeStruct((n_rows, ROUTER_LANES), F32))
    return in_specs, out_specs, out_shape


def _epilogue_args(x, gpost, mod_post, gpre, mod_pre, router):
    args = [x, gpost, mod_post]
    if gpre is not None:
        args += [gpre, mod_pre, mod_pre]
    if router is not None:
        args += list(router)
    return args


def _prenorm_kernel(x_ref, g_ref, sc_ref, sh_ref, h_ref):
    h = _rms(x_ref[...], g_ref[...]) * (1.0 + sc_ref[...]) + sh_ref[...]
    h_ref[...] = h.astype(h_ref.dtype)


def _prenorm(x, g, mod_l, which_pre, h_dtype):
    tm = 512
    modmap = lambda which: (lambda i: (_mod_row(i * tm) * N_MOD + which, 0, 0))
    return pl.pallas_call(
        _prenorm_kernel,
        grid=(N_TOK // tm,),
        in_specs=[pl.BlockSpec((tm, D), lambda i: (i, 0)), pl.BlockSpec((1, D), lambda i: (0, 0)),
                  pl.BlockSpec((None, 1, D), modmap(which_pre + 1)),
                  pl.BlockSpec((None, 1, D), modmap(which_pre))],
        out_specs=pl.BlockSpec((tm, D), lambda i: (i, 0)),
        out_shape=jax.ShapeDtypeStruct((N_TOK, D), h_dtype),
        compiler_params=_cparams(("parallel",)),
        name="prenorm",
    )(x, g, mod_l, mod_l)


def _mm_ws_kernel(h_ref, *refs, glu):
    if glu:
        wa_ref, wb_ref, o_ref, wa_s, wb_s = refs
    else:
        wa_ref, o_ref, wa_s = refs

    @pl.when(pl.program_id(1) == 0)
    def _():
        wa_s[...] = wa_ref[...].astype(BF16)
        if glu:
            wb_s[...] = wb_ref[...].astype(BF16)

    h = h_ref[...]
    a = _dot(h, wa_s[...])
    if glu:
        a = a * _sigmoid(_dot(h, wb_s[...]))
    o_ref[...] = a.astype(o_ref.dtype)


def _mm_ws(h, w, n_out, glu, tm, tn, out_dtype, name):
    k = h.shape[1]
    nj = n_out // tn
    in_specs = [pl.BlockSpec((tm, k), lambda j, i: (i, 0)), pl.BlockSpec((k, tn), lambda j, i: (0, j))]
    args = [h, w]
    scratch = [pltpu.VMEM((k, tn), BF16)]
    if glu:
        in_specs.append(pl.BlockSpec((k, tn), lambda j, i: (0, nj + j)))
        args.append(w)
        scratch.append(pltpu.VMEM((k, tn), BF16))
    return pl.pallas_call(
        functools.partial(_mm_ws_kernel, glu=glu),
        grid=(nj, h.shape[0] // tm),
        in_specs=in_specs,
        out_specs=pl.BlockSpec((tm, tn), lambda j, i: (i, j)),
        out_shape=jax.ShapeDtypeStruct((h.shape[0], n_out), out_dtype),
        scratch_shapes=scratch,
        compiler_params=_cparams(("parallel", "arbitrary")),
        name=name,
    )(*args)


def _mm_post_kernel(ap_ref, as_ref, w_ref, *refs, nk, tiles_p, has_pre, has_router):
    acc_ref = refs[-1]
    i = pl.program_id(0)
    k = pl.program_id(1)

    @pl.when(k == 0)
    def _():
        acc_ref[...] = jnp.zeros_like(acc_ref)

    @pl.when(i < tiles_p)
    def _():
        acc_ref[...] += _dot(ap_ref[...], w_ref[...])

    @pl.when(i >= tiles_p)
    def _():
        acc_ref[...] += _dot(as_ref[...], w_ref[...])

    @pl.when(k == nk - 1)
    def _():
        _epilogue_from_refs(acc_ref[...], refs[:-1], has_pre, has_router)


def _mm_post(a_p, a_s, w, x, gpost, mod_l, which_gt, gpre, which_pre, router, h_dtype, name):
    tm, tk = 512, 1024
    kdim = w.shape[0]
    nk = kdim // tk
    tiles_p = N_PROMPT // tm
    has_pre, has_router = gpre is not None, router is not None
    in_specs, out_specs, out_shape = _epilogue_specs(tm, lambda i, k: i, which_gt, which_pre, has_pre,
                                                     has_router, h_dtype)
    return pl.pallas_call(
        functools.partial(_mm_post_kernel, nk=nk, tiles_p=tiles_p, has_pre=has_pre, has_router=has_router),
        grid=(N_TOK // tm, nk),
        in_specs=[pl.BlockSpec((tm, tk), lambda i, k: (jnp.minimum(i, tiles_p - 1), k)),
                  pl.BlockSpec((tm, tk), lambda i, k: (jnp.maximum(i - tiles_p, 0), k)),
                  pl.BlockSpec((tk, D), lambda i, k: (k, 0))] + in_specs,
        out_specs=out_specs, out_shape=out_shape,
        scratch_shapes=[pltpu.VMEM((tm, D), F32)],
        compiler_params=_cparams(("parallel", "arbitrary")),
        name=name,
    )(a_p, a_s, w, *_epilogue_args(x, gpost, mod_l, gpre, mod_l, router))


def _ret_kernel(lg_ref, q_ref, k_ref, v_ref, g_ref, gn_ref, *refs, seq, positional, has_init, write_state):
    it = iter(refs)
    cos_ref = sin_ref = s0_ref = sf_ref = None
    if positional:
        cos_ref, sin_ref = next(it), next(it)
    if has_init:
        s0_ref = next(it)
    y_ref = next(it)
    if write_state:
        sf_ref = next(it)
    qs, ks, yf, yb, stf, stb = (next(it) for _ in range(6))

    hd = pl.program_id(1)
    nchunk = seq // CHUNK
    lgf = lg_ref[0, hd]
    lgb = lg_ref[1, hd]

    def rot(x_ref_, c0):
        x = x_ref_[pl.ds(c0, CHUNK), :]
        if not positional:
            return x
        cs = cos_ref[pl.ds(c0, CHUNK), :]
        sn = sin_ref[pl.ds(c0, CHUNK), :]
        half = DK // 2
        parts = []
        for p in range(2):
            xp = x[:, p * half:(p + 1) * half]
            rolled = pltpu.roll(xp, half // 2, 1)
            parts.append(xp * cs[:, p * half:(p + 1) * half] + rolled * sn[:, p * half:(p + 1) * half])
        return jnp.concatenate(parts, axis=1)

    def prep(c, carry):
        c0 = pl.multiple_of(c * CHUNK, CHUNK)
        qs[pl.ds(c0, CHUNK), :] = rot(q_ref, c0)
        ks[pl.ds(c0, CHUNK), :] = rot(k_ref, c0) * (DK ** -0.5)
        return carry

    lax.fori_loop(0, nchunk, prep, 0)

    row = lax.broadcasted_iota(jnp.int32, (CHUNK, CHUNK), 0)
    col = lax.broadcasted_iota(jnp.int32, (CHUNK, CHUNK), 1)
    diff = (row - col).astype(F32)
    li = lax.broadcasted_iota(jnp.int32, (CHUNK, 1), 0).astype(F32)

    def decays(direction, lg):
        if direction == 0:
            intra = jnp.where(diff >= 0, jnp.exp(lg * jnp.maximum(diff, 0.0)), 0.0)
            q_dec = jnp.exp(lg * (li + 1.0))
            k_dec = jnp.exp(lg * (CHUNK - 1.0 - li))
        else:
            intra = jnp.where(diff <= 0, jnp.exp(lg * jnp.maximum(-diff, 0.0)), 0.0)
            q_dec = jnp.exp(lg * (CHUNK - li))
            k_dec = jnp.exp(lg * li)
        return intra, q_dec, k_dec, jnp.exp(jnp.full((1, 1), lg, F32) * CHUNK)

    dirs = ((0, stf, yf, decays(0, lgf)), (1, stb, yb, decays(1, lgb)))
    for direction, st, _, _ in dirs:
        if has_init:
            st[...] = s0_ref[direction]
        else:
            st[...] = jnp.zeros_like(st)

    def step(c, st, yout, dec):
        intra, q_dec, k_dec, chunk_dec = dec
        rows = slice(c * CHUNK, (c + 1) * CHUNK)
        qc = qs[rows, :]
        kc = ks[rows, :]
        vc = v_ref[rows, :].astype(BF16)
        state = st[...]
        scores = lax.dot_general(qc.astype(BF16), kc.astype(BF16), (((1,), (1,)), ((), ())),
                                 preferred_element_type=F32) * intra
        yout[rows, :] = _dot(scores.astype(BF16), vc) + _dot((qc * q_dec).astype(BF16), state.astype(BF16))
        st[...] = state * chunk_dec + lax.dot_general(
            (kc * k_dec).astype(BF16), vc, (((0,), (0,)), ((), ())), preferred_element_type=F32)

    for i in range(nchunk):
        step(i, stf, yf, dirs[0][3])
        step(nchunk - 1 - i, stb, yb, dirs[1][3])
    if write_state:
        sf_ref[0] = stf[...]
        sf_ref[1] = stb[...]

    def finish(c, carry):
        c0 = pl.multiple_of(c * CHUNK, CHUNK)
        y = yf[pl.ds(c0, CHUNK), :] + yb[pl.ds(c0, CHUNK), :]
        mu = jnp.mean(y, axis=-1, keepdims=True)
        yc = y - mu
        var = jnp.mean(yc * yc, axis=-1, keepdims=True)
        g = g_ref[pl.ds(c0, CHUNK), :]
        o = (g * _sigmoid(g)) * (yc * lax.rsqrt(var + EPS) * gn_ref[...])
        y_ref[pl.ds(c0, CHUNK), :] = o.astype(y_ref.dtype)
        return carry

    lax.fori_loop(0, nchunk, finish, 0)


def _retention(proj, log_g, gn, rope, init_state, row_off, nb, seq, positional, write_state):
    rb = row_off // seq
    has_init = init_state is not None
    in_specs = [
        pl.BlockSpec(memory_space=pltpu.SMEM),
        pl.BlockSpec((seq, DK), lambda b, h: (rb + b, h)),
        pl.BlockSpec((seq, DK), lambda b, h: (rb + b, HEADS + h)),
        pl.BlockSpec((seq, DV), lambda b, h: (rb + b, HEADS + h)),
        pl.BlockSpec((seq, DV), lambda b, h: (rb + b, 2 * HEADS + h)),
        pl.BlockSpec((1, DV), lambda b, h: (0, h)),
    ]
    args = [log_g, proj, proj, proj, proj, gn]
    if positional:
        in_specs += [pl.BlockSpec((seq, DK), lambda b, h: (0, 0))] * 2
        args += list(rope)
    if has_init:
        in_specs.append(pl.BlockSpec((None, 2, None, DK, DV), lambda b, h: (b, 0, h, 0, 0)))
        args.append(init_state)
    out_specs = [pl.BlockSpec((seq, DV), lambda b, h: (b, h))]
    out_shape = [jax.ShapeDtypeStruct((nb * seq, VD), BF16)]
    if write_state:
        out_specs.append(pl.BlockSpec((None, 2, None, DK, DV), lambda b, h: (b, 0, h, 0, 0)))
        out_shape.append(jax.ShapeDtypeStruct((nb, 2, HEADS, DK, DV), F32))
    res = pl.pallas_call(
        functools.partial(_ret_kernel, seq=seq, positional=positional, has_init=has_init,
                          write_state=write_state),
        grid=(nb, HEADS),
        in_specs=in_specs, out_specs=out_specs, out_shape=out_shape,
        scratch_shapes=[pltpu.VMEM((seq, DK), F32), pltpu.VMEM((seq, DK), F32),
                        pltpu.VMEM((seq, DV), F32), pltpu.VMEM((seq, DV), F32),
                        pltpu.VMEM((DK, DV), F32), pltpu.VMEM((DK, DV), F32)],
        compiler_params=_cparams(("parallel", "parallel")),
        name="retention_s%d" % seq,
    )(*args)
    return res


def _rope_tables(seq):
    half = DK // 4
    s = jnp.arange(seq, dtype=jnp.int32)
    freqs = ROPE_BASE ** (-jnp.arange(half, dtype=F32) / half)
    tabs_c, tabs_s = [], []
    for pos in ((s // GRID_W).astype(F32), (s % GRID_W).astype(F32)):
        ang = pos[:, None] * freqs[None, :]
        c, sn = jnp.cos(ang), jnp.sin(ang)
        tabs_c += [c, c]
        tabs_s += [-sn, sn]
    return jnp.concatenate(tabs_c, axis=1), jnp.concatenate(tabs_s, axis=1)


def _dft_mats(n):
    j = jnp.arange(n, dtype=jnp.int32)
    ang = ((j[:, None] * j[None, :]) % n).astype(F32) * (2.0 * np.pi / n)
    return jnp.cos(ang), jnp.sin(ang)


def _dft1_kernel(h_ref, w_ref, a_ref, b_ref):
    w = w_ref[...]
    for g in range(FNET_GROUPS):
        sl = slice(g * FNET_C, (g + 1) * FNET_C)
        r = _dot(h_ref[:, sl], w)
        a_ref[:, sl] = r[:, :FNET_C].astype(a_ref.dtype)
        b_ref[:, sl] = r[:, FNET_C:].astype(b_ref.dtype)


def _dft1(h, wcs):
    tm = 512
    return pl.pallas_call(
        _dft1_kernel,
        grid=(N_TOK // tm,),
        in_specs=[pl.BlockSpec((tm, D), lambda i: (i, 0)),
                  pl.BlockSpec((FNET_C, 2 * FNET_C), lambda i: (0, 0))],
        out_specs=[pl.BlockSpec((tm, D), lambda i: (i, 0))] * 2,
        out_shape=[jax.ShapeDtypeStruct((N_TOK, D), BF16)] * 2,
        compiler_params=_cparams(("parallel",)),
        name="dft_channels",
    )(h, wcs)


def _dft2_kernel(c_ref, s_ref, a_ref, b_ref, f_ref, *, scale):
    f = _dot(c_ref[...], a_ref[...]) - _dot(s_ref[...], b_ref[...])
    f_ref[...] = (f * scale).astype(f_ref.dtype)


def _dft2(a, b, cmat, smat, row_off, nb, seq):
    tr = 256
    nr = seq // tr
    sb = row_off // seq
    return pl.pallas_call(
        functools.partial(_dft2_kernel, scale=float(1.0 / np.sqrt(seq * FNET_C))),
        grid=(nb, nr),
        in_specs=[pl.BlockSpec((tr, seq), lambda bi, r: (r, 0)),
                  pl.BlockSpec((tr, seq), lambda bi, r: (r, 0)),
                  pl.BlockSpec((seq, D), lambda bi, r: (sb + bi, 0)),
                  pl.BlockSpec((seq, D), lambda bi, r: (sb + bi, 0))],
        out_specs=pl.BlockSpec((tr, D), lambda bi, r: (bi * nr + r, 0)),
        out_shape=jax.ShapeDtypeStruct((nb * seq, D), BF16),
        compiler_params=_cparams(("parallel", "parallel")),
        name="dft_positions_s%d" % seq,
    )(cmat, smat, a, b)


CONV_HALO = 16
CONV_PAD = CONV_WIDTH // 2


def _dwconv_kernel(prev_ref, cur_ref, next_ref, w_ref, b_ref, lg_ref, lb_ref, o_ref, win, cv, *, ts, nr):
    r = pl.program_id(1)
    zero = jnp.zeros((CONV_HALO, D), F32)
    win[0:CONV_HALO, :] = jnp.where(r > 0, prev_ref[...], zero)
    win[CONV_HALO:CONV_HALO + ts, :] = cur_ref[...]
    win[CONV_HALO + ts:, :] = jnp.where(r < nr - 1, next_ref[...], zero)
    lanes = 256
    rows = 128

    def col_body(cj, carry):
        c0 = pl.multiple_of(cj * lanes, lanes)
        for r0 in range(0, ts, rows):
            acc = jnp.broadcast_to(b_ref[:, pl.ds(c0, lanes)], (rows, lanes))
            for t in range(CONV_WIDTH):
                start = CONV_HALO - CONV_PAD + t + r0
                acc = acc + w_ref[t:t + 1, pl.ds(c0, lanes)] * win[start:start + rows, pl.ds(c0, lanes)]
            cv[r0:r0 + rows, pl.ds(c0, lanes)] = acc
        return carry

    lax.fori_loop(0, D // lanes, col_body, 0)
    u = cv[...]
    mu = jnp.mean(u, axis=-1, keepdims=True)
    uc = u - mu
    var = jnp.mean(uc * uc, axis=-1, keepdims=True)
    y = uc * lax.rsqrt(var + EPS) * lg_ref[...] + lb_ref[...]
    o_ref[...] = (y * _sigmoid(y)).astype(o_ref.dtype)


def _dwconv(u, wdw, bdw, ln_g, ln_b, row_off, nb, seq):
    ts = 256
    nr = seq // ts
    hb = ts // CONV_HALO
    tb = row_off // ts
    n_hblocks = N_TOK // CONV_HALO

    def cur(bi, r):
        return (tb + bi * nr + r, 0)

    def prev(bi, r):
        return (jnp.maximum((tb + bi * nr + r) * hb - 1, 0), 0)

    def nxt(bi, r):
        return (jnp.minimum((tb + bi * nr + r + 1) * hb, n_hblocks - 1), 0)

    const = lambda bi, r: (0, 0)
    return pl.pallas_call(
        functools.partial(_dwconv_kernel, ts=ts, nr=nr),
        grid=(nb, nr),
        in_specs=[pl.BlockSpec((CONV_HALO, D), prev), pl.BlockSpec((ts, D), cur),
                  pl.BlockSpec((CONV_HALO, D), nxt),
                  pl.BlockSpec((CONV_WIDTH + 1, D), const), pl.BlockSpec((1, D), const),
                  pl.BlockSpec((1, D), const), pl.BlockSpec((1, D), const)],
        out_specs=pl.BlockSpec((ts, D), lambda bi, r: (bi * nr + r, 0)),
        out_shape=jax.ShapeDtypeStruct((nb * seq, D), BF16),
        scratch_shapes=[pltpu.VMEM((ts + 2 * CONV_HALO, D), F32), pltpu.VMEM((ts, D), F32)],
        compiler_params=_cparams(("parallel", "parallel")),
        name="dwconv_s%d" % seq,
    )(u, u, u, wdw, bdw, ln_g, ln_b)


POOL_HALO = 8


def _pool_kernel(prev_ref, cur_ref, next_ref, w_ref, ps_ref, *refs, ts, seq, nr):
    win, yv = refs[-2], refs[-1]
    r = pl.program_id(1)
    win[0:POOL_HALO, :] = prev_ref[...]
    win[POOL_HALO:POOL_HALO + ts, :] = cur_ref[...]
    win[POOL_HALO + ts:, :] = next_ref[...]
    pos = r * ts + lax.broadcasted_iota(jnp.int32, (ts, 1), 0)
    for gi, wd in enumerate(POOL_WINDOWS):
        sl = slice(gi * POOL_GROUP, (gi + 1) * POOL_GROUP)
        half = wd // 2
        acc = jnp.zeros((ts, POOL_GROUP), F32)
        cnt = jnp.zeros((ts, 1), F32)
        for dlt in range(-half, half):
            ok = jnp.logical_and(pos + dlt >= 0, pos + dlt <= seq - 1)
            acc = acc + jnp.where(ok, win[POOL_HALO + dlt:POOL_HALO + dlt + ts, sl], 0.0)
            cnt = cnt + jnp.where(ok, 1.0, 0.0)
        pooled = acc / cnt - cur_ref[:, sl]
        yv[:, sl] = _dot(pooled.astype(BF16), w_ref[gi].astype(BF16))
    out = yv[...] * ps_ref[...]
    _epilogue_from_refs(out, refs[:-2], True, True)


def _pool(h, x, w_pool, pscale, gpost, mod_l, gpre, router, row_off, nb, seq):
    ts = 256
    nr = seq // ts
    hb = ts // POOL_HALO
    tb = row_off // ts
    n_hblocks = N_TOK // POOL_HALO

    def tile(bi, r):
        return tb + bi * nr + r

    def cur(bi, r):
        return (tile(bi, r), 0)

    def prev(bi, r):
        return (jnp.maximum(tile(bi, r) * hb - 1, 0), 0)

    def nxt(bi, r):
        return (jnp.minimum((tile(bi, r) + 1) * hb, n_hblocks - 1), 0)

    ep_in, ep_out, ep_shape = _epilogue_specs(ts, tile, 2, 3, True, True, BF16)
    ep_out = [pl.BlockSpec(s.block_shape, lambda bi, r: (bi * nr + r, 0)) for s in ep_out]
    ep_shape = [jax.ShapeDtypeStruct((nb * seq,) + s.shape[1:], s.dtype) for s in ep_shape]
    const = lambda bi, r: (0, 0)
    return pl.pallas_call(
        functools.partial(_pool_kernel, ts=ts, seq=seq, nr=nr),
        grid=(nb, nr),
        in_specs=[pl.BlockSpec((POOL_HALO, D), prev), pl.BlockSpec((ts, D), cur),
                  pl.BlockSpec((POOL_HALO, D), nxt),
                  pl.BlockSpec((len(POOL_WINDOWS), POOL_GROUP, POOL_GROUP), lambda bi, r: (0, 0, 0)),
                  pl.BlockSpec((1, D), const)] + ep_in,
        out_specs=ep_out, out_shape=ep_shape,
        scratch_shapes=[pltpu.VMEM((ts + 2 * POOL_HALO, D), F32), pltpu.VMEM((ts, D), F32)],
        compiler_params=_cparams(("parallel", "parallel")),
        name="pool_s%d" % seq,
    )(h, h, h, w_pool, pscale, *_epilogue_args(x, gpost, mod_l, gpre, mod_l, router))


FFN_ROWS = 512


def _ffn_up_kernel(xe_ref, wg_ref, wu_ref, hid_ref, wg_s, wu_s):
    wg_s[...] = wg_ref[...].astype(BF16)
    wu_s[...] = wu_ref[...].astype(BF16)

    def body(i, carry):
        r0 = pl.multiple_of(i * FFN_ROWS, FFN_ROWS)
        xe = xe_ref[pl.ds(r0, FFN_ROWS), :]
        a = _dot(xe, wg_s[...])
        b = _dot(xe, wu_s[...])
        hid_ref[pl.ds(r0, FFN_ROWS), :] = ((a * _sigmoid(a)) * b).astype(hid_ref.dtype)
        return carry

    lax.fori_loop(0, CAP_T // FFN_ROWS, body, 0, unroll=True)


def _ffn_down_kernel(hid_ref, wd_ref, ye_ref, wd_s):
    wd_s[...] = wd_ref[...].astype(BF16)

    def body(i, carry):
        r0 = pl.multiple_of(i * FFN_ROWS, FFN_ROWS)
        y = _dot(hid_ref[pl.ds(r0, FFN_ROWS), :], wd_s[...])
        ye_ref[pl.ds(r0, FFN_ROWS), :] = y.astype(ye_ref.dtype)
        return carry

    lax.fori_loop(0, CAP_T // FFN_ROWS, body, 0, unroll=True)


def _expert_ffn(xe, layer, wg, wu, wd):
    tf, td = 512, 512
    hid = pl.pallas_call(
        _ffn_up_kernel,
        grid=(N_EXPERTS, D_FF // tf),
        in_specs=[pl.BlockSpec((None, CAP_T, D), lambda e, f: (e, 0, 0)),
                  pl.BlockSpec((None, None, D, tf), lambda e, f: (layer, e, 0, f)),
                  pl.BlockSpec((None, None, D, tf), lambda e, f: (layer, e, 0, f))],
        out_specs=pl.BlockSpec((None, CAP_T, tf), lambda e, f: (e, 0, f)),
        out_shape=jax.ShapeDtypeStruct((N_EXPERTS, CAP_T, D_FF), BF16),
        scratch_shapes=[pltpu.VMEM((D, tf), BF16), pltpu.VMEM((D, tf), BF16)],
        compiler_params=_cparams(("parallel", "parallel")),
        name="ffn_up",
    )(xe, wg, wu)
    return pl.pallas_call(
        _ffn_down_kernel,
        grid=(N_EXPERTS, D // td),
        in_specs=[pl.BlockSpec((None, CAP_T, D_FF), lambda e, j: (e, 0, 0)),
                  pl.BlockSpec((None, None, D_FF, td), lambda e, j: (layer, e, 0, j))],
        out_specs=pl.BlockSpec((None, CAP_T, td), lambda e, j: (e, 0, j)),
        out_shape=jax.ShapeDtypeStruct((N_EXPERTS, CAP_T, D), BF16),
        scratch_shapes=[pltpu.VMEM((D_FF, td), BF16)],
        compiler_params=_cparams(("parallel", "parallel")),
        name="ffn_down",
    )(hid, wd)


RB = 256
N_RB = N_TOK // RB


def _route_kernel(aff_ref, slot_ref, base_ref, cnt_ref, key_s, *, n, cap, slot_off):
    nb = n // RB
    key_s[...] = pltpu.bitcast(aff_ref[...], jnp.int32)

    def bit_body(i, t):
        cand = t | jnp.left_shift(jnp.int32(1), 30 - i)
        cnt = jnp.sum((key_s[...] >= cand).astype(jnp.int32), axis=0, keepdims=True)
        return jnp.where(cnt >= cap, cand, t)

    t = lax.fori_loop(0, 31, bit_body, jnp.zeros((1, ROUTER_LANES), jnp.int32))
    c_gt = jnp.sum((key_s[...] > t).astype(jnp.int32), axis=0, keepdims=True)
    need = (cap - c_gt).astype(F32)
    lane_ok = lax.broadcasted_iota(jnp.int32, (1, ROUTER_LANES), 1) < N_EXPERTS
    tri = (lax.broadcasted_iota(jnp.int32, (RB, RB), 0) >
           lax.broadcasted_iota(jnp.int32, (RB, RB), 1)).astype(BF16)

    def blk(b, carry):
        eq_carry, sel_carry = carry
        r0 = pl.multiple_of(b * RB, RB)
        k = key_s[pl.ds(r0, RB), :]
        eq = k == t
        eq_rank = _dot(tri, eq.astype(BF16)) + eq_carry
        sel = jnp.logical_or(k > t, jnp.logical_and(eq, eq_rank < need))
        self_f = sel.astype(F32)
        pos = _dot(tri, sel.astype(BF16)) + sel_carry
        slot = jnp.where(jnp.logical_and(sel, lane_ok), pos.astype(jnp.int32) + slot_off, -1)
        slot_ref[pl.ds(r0, RB), :] = slot
        n_sel = jnp.sum(self_f, axis=0, keepdims=True)
        base_ref[pl.ds(b, 1), :] = sel_carry.astype(jnp.int32) + slot_off
        cnt_ref[pl.ds(b, 1), :] = n_sel.astype(jnp.int32)
        return eq_carry + jnp.sum(eq.astype(F32), axis=0, keepdims=True), sel_carry + n_sel

    zero = jnp.zeros((1, ROUTER_LANES), F32)
    lax.fori_loop(0, nb, blk, (zero, zero))


def _route(aff_g, cap, slot_off):
    n = aff_g.shape[0]
    return pl.pallas_call(
        functools.partial(_route_kernel, n=n, cap=cap, slot_off=slot_off),
        out_shape=[jax.ShapeDtypeStruct((n, ROUTER_LANES), jnp.int32),
                   jax.ShapeDtypeStruct((n // RB, ROUTER_LANES), jnp.int32),
                   jax.ShapeDtypeStruct((n // RB, ROUTER_LANES), jnp.int32)],
        scratch_shapes=[pltpu.VMEM((n, ROUTER_LANES), jnp.int32)],
        compiler_params=pltpu.CompilerParams(vmem_limit_bytes=VMEM_LIMIT),
        name="route_n%d" % n,
    )(aff_g)


GATHER_G = 2
GATHER_WIN_LOG2 = 6
GATHER_WIN = 1 << GATHER_WIN_LOG2
GATHER_STACK = 2


def _gather_kernel(base_ref, cnt_ref, slot_ref, h_ref, xe_ref):
    eg, j = pl.program_id(0), pl.program_id(1)

    @pl.when(j == 0)
    def _():
        xe_ref[...] = jnp.zeros_like(xe_ref)

    sio = lax.broadcasted_iota(jnp.int32, (GATHER_WIN, RB), 0)

    def onehot(tok_slot, r0):
        return (tok_slot - r0 == sio).astype(BF16)

    runs = []
    for u in range(GATHER_G):
        e = eg * GATHER_G + u
        base = base_ref[e * N_RB + j]
        cnt = cnt_ref[e * N_RB + j]
        first = base & (-GATHER_WIN)
        n_win = jnp.where(cnt > 0,
                          lax.shift_right_logical(base + cnt - first + GATHER_WIN - 1, GATHER_WIN_LOG2), 0)
        runs.append((u, first, n_win, slot_ref[pl.ds(e, 1), :]))

    stacked = jnp.concatenate([onehot(ts, first + w * GATHER_WIN)
                               for (_, first, _, ts) in runs for w in range(GATHER_STACK)], axis=0)
    rows = _dot(stacked, h_ref[...]).astype(xe_ref.dtype)
    for (u, first, n_win, ts) in runs:
        for w in range(GATHER_STACK):
            r0 = pl.multiple_of(jnp.minimum(first + w * GATHER_WIN, CAP_T - GATHER_WIN), GATHER_WIN)
            i0 = (u * GATHER_STACK + w) * GATHER_WIN
            xe_ref[u, pl.ds(r0, GATHER_WIN), :] += rows[i0:i0 + GATHER_WIN]

        def window(w, carry):
            r0 = pl.multiple_of(first + w * GATHER_WIN, GATHER_WIN)
            xe_ref[u, pl.ds(r0, GATHER_WIN), :] += _dot(onehot(ts, r0), h_ref[...]).astype(xe_ref.dtype)
            return carry

        lax.fori_loop(GATHER_STACK, n_win, window, 0)


def _gather(h, slot_t, base_tbl, cnt_tbl):
    return pl.pallas_call(
        _gather_kernel,
        grid_spec=pltpu.PrefetchScalarGridSpec(
            num_scalar_prefetch=2,
            grid=(N_EXPERTS // GATHER_G, N_RB),
            in_specs=[pl.BlockSpec((N_EXPERTS, RB), lambda eg, j, b, c: (0, j)),
                      pl.BlockSpec((RB, D), lambda eg, j, b, c: (j, 0))],
            out_specs=pl.BlockSpec((GATHER_G, CAP_T, D), lambda eg, j, b, c: (eg, 0, 0)),
        ),
        out_shape=jax.ShapeDtypeStruct((N_EXPERTS, CAP_T, D), BF16),
        compiler_params=_cparams(("parallel", "arbitrary")),
        name="moe_gather",
    )(base_tbl, cnt_tbl, slot_t, h)


BF16_EXACT_LOG2 = 8
CHUNK_LOG2 = 5
CHUNK_SLOTS = 1 << CHUNK_LOG2
GROUP_LOG2 = 3
CHUNKS_PER_GROUP = 1 << GROUP_LOG2
assert CHUNKS_PER_GROUP * CHUNK_SLOTS == RB
MAX_CHUNKS = N_EXPERTS * (RB // CHUNK_SLOTS + 1)


def _combine_kernel(nch_ref, che_ref, chc_ref, slot_ref, gate_ref, ye_hbm, *refs, has_pre):
    stage, sems, y_acc = refs[-3], refs[-2], refs[-1]
    j = pl.program_id(0)
    n = nch_ref[j]
    n_groups = lax.shift_right_logical(n + CHUNKS_PER_GROUP - 1, GROUP_LOG2)

    @pl.when(j == 0)
    def _():
        stage[...] = jnp.zeros_like(stage)

    def chunk_copy(g, q, buf):
        k = j * MAX_CHUNKS + g * CHUNKS_PER_GROUP + q
        src = ye_hbm.at[che_ref[k], pl.ds(pl.multiple_of(chc_ref[k] * CHUNK_SLOTS, CHUNK_SLOTS), CHUNK_SLOTS), :]
        return pltpu.make_async_copy(src, stage.at[buf, pl.ds(q * CHUNK_SLOTS, CHUNK_SLOTS), :], sems.at[buf])

    def for_group(g, buf, fn):
        for q in range(CHUNKS_PER_GROUP):
            @pl.when(g * CHUNKS_PER_GROUP + q < n)
            def _():
                fn(chunk_copy(g, q, buf))

    slot1 = slot_ref[...] + 1
    s_hi = lax.shift_right_logical(slot1, BF16_EXACT_LOG2).astype(F32).astype(BF16)
    s_lo = (slot1 & ((1 << BF16_EXACT_LOG2) - 1)).astype(F32).astype(BF16)
    gate = gate_ref[...]
    g_hi = gate.astype(BF16)
    g_lo = (gate - g_hi.astype(F32)).astype(BF16)
    lane = lax.broadcasted_iota(jnp.int32, (1, RB), 1)
    pos_in_group = lax.shift_right_logical(lane, CHUNK_LOG2)
    expert_iota = lax.broadcasted_iota(jnp.int32, (ROUTER_LANES, RB), 0)
    y_acc[...] = jnp.zeros_like(y_acc)

    @pl.when(n_groups > 0)
    def _():
        for_group(0, 0, lambda cp: cp.start())

    def group_body(g, carry):
        buf = g & 1

        @pl.when(g + 1 < n_groups)
        def _():
            for_group(g + 1, 1 - buf, lambda cp: cp.start())

        for_group(g, buf, lambda cp: cp.wait())
        want = jnp.full((1, RB), -1, jnp.int32)
        owner = jnp.zeros((1, RB), jnp.int32)
        for q in range(CHUNKS_PER_GROUP):
            kq = g * CHUNKS_PER_GROUP + q
            k = j * MAX_CHUNKS + kq
            first = chc_ref[k] * CHUNK_SLOTS + 1 - q * CHUNK_SLOTS
            want = jnp.where(pos_in_group == q, jnp.where(kq < n, lane + first, -1), want)
            owner = jnp.where(pos_in_group == q, che_ref[k], owner)
        pick = (expert_iota == owner).astype(BF16)
        tok_slot = float(1 << BF16_EXACT_LOG2) * _dot(s_hi, pick) + _dot(s_lo, pick)
        tok_gate = _dot(g_hi, pick) + _dot(g_lo, pick)
        w = jnp.where(tok_slot == want.astype(F32), tok_gate, 0.0)
        w_hi = w.astype(BF16)
        w_lo = (w - w_hi.astype(F32)).astype(BF16)
        rows = stage[buf]
        y_acc[...] += _dot(w_hi, rows) + _dot(w_lo, rows)
        return carry

    lax.fori_loop(0, n_groups, group_body, 0)
    _epilogue_from_refs(y_acc[...], refs[:-3], has_pre, False)


def _chunk_tables(base_tbl, cnt_tbl):
    base = base_tbl.reshape(N_EXPERTS, N_RB).T
    cnt = cnt_tbl.reshape(N_EXPERTS, N_RB).T
    first = lax.shift_right_logical(base, CHUNK_LOG2)
    last = lax.shift_right_logical(base + cnt - 1, CHUNK_LOG2)
    per_expert = jnp.where(cnt > 0, last - first + 1, 0)
    ends = jnp.cumsum(per_expert, axis=1)
    starts = ends - per_expert
    k = jnp.arange(MAX_CHUNKS, dtype=jnp.int32)[None, :, None]
    mine = jnp.logical_and(k >= starts[:, None, :], k < ends[:, None, :])
    expert = jnp.arange(N_EXPERTS, dtype=jnp.int32)[None, None, :]
    owner = jnp.sum(jnp.where(mine, expert, 0), axis=2)
    chunk = jnp.sum(jnp.where(mine, first[:, None, :] + k - starts[:, None, :], 0), axis=2)
    return ends[:, -1], owner.reshape(-1), chunk.reshape(-1)


def _combine_post(ye, slot, aff, base_tbl, cnt_tbl, x, gpost, mod_l, gpre, mod_pre, h_dtype):
    nch, che, chc = _chunk_tables(base_tbl, cnt_tbl)
    has_pre = gpre is not None
    tok = lambda j, *_: (j, 0)
    ep_in, ep_out, ep_shape = _epilogue_specs(RB, lambda j, *_: j, 5, 0, has_pre, False, h_dtype)
    res = pl.pallas_call(
        functools.partial(_combine_kernel, has_pre=has_pre),
        grid_spec=pltpu.PrefetchScalarGridSpec(
            num_scalar_prefetch=3,
            grid=(N_RB,),
            in_specs=[pl.BlockSpec((RB, ROUTER_LANES), tok),
                      pl.BlockSpec((RB, ROUTER_LANES), tok),
                      pl.BlockSpec(memory_space=pl.ANY)] + ep_in,
            out_specs=ep_out,
            scratch_shapes=[pltpu.VMEM((2, RB, D), BF16), pltpu.SemaphoreType.DMA((2,)),
                            pltpu.VMEM((RB, D), F32)],
        ),
        out_shape=ep_shape,
        compiler_params=_cparams(("arbitrary",)),
        name="moe_combine",
    )(nch, che, chc, slot, aff, ye, *_epilogue_args(x, gpost, mod_l, gpre, mod_pre, None))
    return res if has_pre else (res[0], None)


def _moe(h, aff, layer, wg, wu, wd, x, gpost, mod_l, gpre, mod_pre, h_dtype):
    slot_p, base_p, cnt_p = _route(aff[:N_PROMPT], CAP_P, 0)
    slot_s, base_s, cnt_s = _route(aff[N_PROMPT:], CAP_S, CAP_P)
    slot = jnp.concatenate([slot_p, slot_s], axis=0)

    def table(tp, ts):
        return jnp.concatenate([tp, ts], axis=0)[:, :N_EXPERTS].T.reshape(-1)

    base_tbl, cnt_tbl = table(base_p, base_s), table(cnt_p, cnt_s)
    xe = _gather(h, slot[:, :N_EXPERTS].T, base_tbl, cnt_tbl)
    ye = _expert_ffn(xe, layer, wg, wu, wd)
    return _combine_post(ye, slot, aff, base_tbl, cnt_tbl, x, gpost, mod_l, gpre, mod_pre, h_dtype)


def kernel(x_prompt, x_sample, state_ret, c, c_ctx, w_mod, b_mod, g_norm, ret_w_in, ret_w_out, ret_decay,
           ret_gn, fnet_w, conv_w1, conv_wdw, conv_bdw, conv_ln_g, conv_ln_b, conv_w2, pool_w, pool_scale,
           moe_router, moe_w_gate, moe_w_up, moe_w_down):
    x = jnp.concatenate([x_prompt.reshape(N_PROMPT, D), x_sample.reshape(N_SAMPLE, D)], axis=0)
    cond_all = jnp.concatenate([c_ctx[None, :], c, jnp.zeros((MOD_ROWS - 1 - DEC_BATCH, D), F32)], axis=0)
    mod = _mod_table(cond_all, w_mod, b_mod)
    groups = ((0, BATCH, SEQ), (N_PROMPT, DEC_BATCH, DEC_SEQ))

    def gn(i, k):
        return g_norm[i, k][None, :]

    def router(i):
        wr = jnp.pad(moe_router[i], ((0, 0), (0, ROUTER_LANES - N_EXPERTS)))
        wrh = wr.astype(BF16)
        wrl = (wr - wrh.astype(F32)).astype(BF16)
        return wrh, wrl

    new_state = None
    h = _prenorm(x, gn(0, 0), mod[0], 0, BF16)
    for i in range(DEPTH):
        mod_l = mod[i]
        kind = i % 4
        if kind == 0:
            proj = _mm_ws(h, ret_w_in[0], IN_W, False, 512, 1024, F32, "ret_in_proj")
            log_g = jax.nn.log_sigmoid(ret_decay[0].astype(F32))
            ys = []
            for (row_off, nb, seq) in groups:
                positional = row_off != 0
                res = _retention(proj, log_g, ret_gn[0][None, :], _rope_tables(seq) if positional else None,
                                 state_ret[:, 0] if positional else None, row_off, nb, seq, positional,
                                 not positional)
                ys.append(res[0])
                if not positional:
                    new_state = res[1]
            x, h, aff = _mm_post(ys[0], ys[1], ret_w_out[0].astype(BF16), x, gn(i, 1), mod_l, 2, gn(i, 2), 3,
                                 router(i), BF16, "ret_out_proj")
        elif kind == 1:
            cc, sc = _dft_mats(FNET_C)
            wcs = jnp.concatenate([cc, sc], axis=1).astype(BF16)
            fa, fb = _dft1(h, wcs)
            fs = []
            for (row_off, nb, seq) in groups:
                cm, sm = _dft_mats(seq)
                fs.append(_dft2(fa, fb, cm.astype(BF16), sm.astype(BF16), row_off, nb, seq))
            x, h, aff = _mm_post(fs[0], fs[1], fnet_w[0].astype(BF16), x, gn(i, 1), mod_l, 2, gn(i, 2), 3,
                                 router(i), BF16, "fnet_proj")
        elif kind == 2:
            u = _mm_ws(h, conv_w1[0], D, True, 512, 512, F32, "conv_glu")
            wdw = jnp.pad(conv_wdw[0], ((0, 1), (0, 0)))
            vs = [_dwconv(u, wdw, conv_bdw[0][None, :], conv_ln_g[0][None, :], conv_ln_b[0][None, :],
                          row_off, nb, seq) for (row_off, nb, seq) in groups]
            x, h, aff = _mm_post(vs[0], vs[1], conv_w2[0].astype(BF16), x, gn(i, 1), mod_l, 2, gn(i, 2), 3,
                                 router(i), BF16, "conv_out_proj")
        else:
            outs = [_pool(h, x, pool_w[0], pool_scale[0][None, :], gn(i, 1), mod_l, gn(i, 2), router(i),
                          row_off, nb, seq) for (row_off, nb, seq) in groups]
            x = jnp.concatenate([o[0] for o in outs], axis=0)
            h = jnp.concatenate([o[1] for o in outs], axis=0)
            aff = jnp.concatenate([o[2] for o in outs], axis=0)
        if i + 1 < DEPTH:
            nxt_dtype = F32 if (i + 1) % 4 == 3 else BF16
            x, h = _moe(h, aff, i, moe_w_gate, moe_w_up, moe_w_down, x, gn(i, 3), mod_l, gn(i + 1, 0),
                        mod[i + 1], nxt_dtype)
        else:
            x, h = _moe(h, aff, i, moe_w_gate, moe_w_up, moe_w_down, x, gn(i, 3), mod_l, None, None, BF16)
    y_prompt = x[:N_PROMPT].reshape(BATCH, SEQ, D)
    y_sample = x[N_PROMPT:].reshape(DEC_BATCH, DEC_SEQ, D)
    return y_prompt, y_sample, new_state[:, None]
```
